```python
import math
import jax, jax.numpy as jnp
from jax import lax
import numpy as np

D_MODEL = 2048
BATCH = 16
SEQ = 2048
DEPTH = 2

MIX_WIDTH = D_MODEL
A_HEAD_DIM = 64
A_WIDTH = 3 * D_MODEL // 8
A_Q_HEADS = A_WIDTH // A_HEAD_DIM
A_KV_HEADS = A_Q_HEADS // 3
A_GROUP = A_Q_HEADS // A_KV_HEADS
A_KV_WIDTH = A_KV_HEADS * A_HEAD_DIM
WINDOW = 128
ATTN_BLOCK = 128
ROPE_THETA = 500000.0
ROT_DIM = A_HEAD_DIM // 4
B_WIDTH = D_MODEL // 4
B_CONV_WIDTH = 31
C_HEAD_DIM = 128
C_HEADS = (MIX_WIDTH - A_WIDTH - B_WIDTH) // C_HEAD_DIM
C_WIDTH = C_HEADS * C_HEAD_DIM
C_CONV_WIDTH = 4
CHUNK = 64

EPS = 1e-6
MAX_POS_OFFSET = 4096

COL_SPLITS = (A_WIDTH, A_KV_WIDTH, A_KV_WIDTH, A_WIDTH,
              2 * B_WIDTH, B_WIDTH,
              3 * C_WIDTH, C_HEADS, C_HEADS, C_WIDTH)
IN_COLS = sum(COL_SPLITS)

kernel_name = "hybrid_swa_conformer_gdn_parallel"


def _split_cols(p):
    idx = []
    s = 0
    for n in COL_SPLITS[:-1]:
        s += n
        idx.append(s)
    return jnp.split(p, idx, axis=-1)


def rms_norm(x, w):
    x32 = x.astype(jnp.float32)
    y = x32 * lax.rsqrt(jnp.mean(x32 * x32, axis=-1, keepdims=True) + EPS)
    return (y * w.astype(jnp.float32)).astype(x.dtype)


def layer_norm(x, w, b):
    x32 = x.astype(jnp.float32)
    mu = jnp.mean(x32, axis=-1, keepdims=True)
    var = jnp.mean(jnp.square(x32 - mu), axis=-1, keepdims=True)
    y = (x32 - mu) * lax.rsqrt(var + EPS)
    return (y * w.astype(jnp.float32) + b.astype(jnp.float32)).astype(x.dtype)


def l2_norm(x):
    return x * lax.rsqrt(jnp.sum(x * x, axis=-1, keepdims=True) + EPS)


def causal_depthwise_conv(x, w):
    K, C = w.shape
    return lax.conv_general_dilated(
        x, w[:, None, :].astype(x.dtype), window_strides=(1,), padding=[(K - 1, 0)],
        dimension_numbers=('NWC', 'WIO', 'NWC'), feature_group_count=C)


def rope_tables(positions):
    inv_freq = ROPE_THETA ** (-jnp.arange(0, ROT_DIM, 2, dtype=jnp.float32) / ROT_DIM)
    ang = positions.astype(jnp.float32)[..., None] * inv_freq
    return jnp.cos(ang)[:, :, None, :], jnp.sin(ang)[:, :, None, :]


def apply_partial_rope(x, cos, sin):
    half = ROT_DIM // 2
    x1 = x[..., :half].astype(jnp.float32)
    x2 = x[..., half:ROT_DIM].astype(jnp.float32)
    rot = jnp.concatenate([x1 * cos - x2 * sin, x2 * cos + x1 * sin], axis=-1)
    return jnp.concatenate([rot.astype(x.dtype), x[..., ROT_DIM:]], axis=-1)


def sliding_window_sink_attention(q, k, v, sinks):
    bsz, T = q.shape[0], q.shape[1]
    nb = T // ATTN_BLOCK
    qb = q.reshape(bsz, nb, ATTN_BLOCK, A_KV_HEADS, A_GROUP, A_HEAD_DIM)
    kb = k.reshape(bsz, nb, ATTN_BLOCK, A_KV_HEADS, A_HEAD_DIM)
    vb = v.reshape(bsz, nb, ATTN_BLOCK, A_KV_HEADS, A_HEAD_DIM)
    prev = lambda t: jnp.concatenate([jnp.zeros_like(t[:, :1]), t[:, :-1]], axis=1)
    kk = jnp.concatenate([prev(kb), kb], axis=2)
    vv = jnp.concatenate([prev(vb), vb], axis=2)
    s = jnp.einsum('bnqhgd,bnkhd->bnhgqk', qb, kk).astype(jnp.float32) * (A_HEAD_DIM ** -0.5)
    qi = jnp.arange(ATTN_BLOCK)[:, None]
    kj = jnp.arange(2 * ATTN_BLOCK)[None, :]
    dist = qi + ATTN_BLOCK - kj
    band = (dist >= 0) & (dist < WINDOW)
    not_pad = (jnp.arange(nb)[:, None, None] > 0) | (kj[None] >= ATTN_BLOCK)
    valid = band[None] & not_pad
    s = jnp.where(valid[None, :, None, None], s, -jnp.inf)
    sink = sinks.reshape(A_KV_HEADS, A_GROUP).astype(jnp.float32)[None, None, :, :, None, None]
    sink = jnp.broadcast_to(sink, s.shape[:-1] + (1,))
    p = jax.nn.softmax(jnp.concatenate([s, sink], axis=-1), axis=-1)[..., :-1]
    o = jnp.einsum('bnhgqk,bnkhd->bnqhgd', p.astype(v.dtype), vv)
    return o.reshape(bsz, T, A_Q_HEADS * A_HEAD_DIM)


def conformer_conv_module(u, conv_w, conv_b, ln_w, ln_b, pw_w, pw_b):
    a, gt = jnp.split(u, 2, axis=-1)
    h = a * jax.nn.sigmoid(gt)
    h = causal_depthwise_conv(h, conv_w) + conv_b.astype(h.dtype)
    h = jax.nn.silu(layer_norm(h, ln_w, ln_b))
    return h @ pw_w + pw_b


def gated_delta_rule_chunked(q, k, v, g, beta):
    bsz, T, H, Dk = q.shape
    Dv = v.shape[-1]
    n = T // CHUNK
    q = q * (Dk ** -0.5)

    def to_chunks(t):
        return t.reshape(bsz, n, CHUNK, H, t.shape[-1]).transpose(0, 1, 3, 2, 4)

    qc, kc, vc = to_chunks(q), to_chunks(k), to_chunks(v)
    bc = beta.reshape(bsz, n, CHUNK, H).transpose(0, 1, 3, 2)
    gc = jnp.cumsum(g.reshape(bsz, n, CHUNK, H).transpose(0, 1, 3, 2), axis=-1)
    idx = jnp.arange(CHUNK)
    incl = idx[:, None] >= idx[None, :]
    strict = idx[:, None] > idx[None, :]
    decay = jnp.exp(jnp.where(incl, gc[..., :, None] - gc[..., None, :], -jnp.inf))
    kb = kc * bc[..., None]
    lower = jnp.where(strict, jnp.einsum('bnhid,bnhjd->bnhij', kb, kc) * decay, 0.0)
    eye = jnp.eye(CHUNK, dtype=lower.dtype)
    rhs = jnp.concatenate([vc * bc[..., None], kb * jnp.exp(gc)[..., None]], axis=-1)
    sol = lax.linalg.triangular_solve(lower + eye, rhs, left_side=True, lower=True,
                                      unit_diagonal=True)
    u, w = sol[..., :Dv], sol[..., Dv:]
    intra = jnp.where(incl, jnp.einsum('bnhid,bnhjd->bnhij', qc, kc) * decay, 0.0)

    def step(S, xs):
        q_i, k_i, u_i, w_i, g_i, a_i = xs
        v_new = u_i - jnp.einsum('bhck,bhkv->bhcv', w_i, S)
        o_i = (jnp.einsum('bhck,bhkv->bhcv', q_i * jnp.exp(g_i)[..., None], S)
               + jnp.einsum('bhij,bhjv->bhiv', a_i, v_new))
        g_last = g_i[..., -1:]
        S = (S * jnp.exp(g_last)[..., None]
             + jnp.einsum('bhck,bhcv->bhkv', k_i * jnp.exp(g_last - g_i)[..., None], v_new))
        return S, o_i

    S0 = jnp.zeros((bsz, H, Dk, Dv), q.dtype)
    xs = tuple(jnp.moveaxis(t, 1, 0) for t in (qc, kc, u, w, gc, intra))
    _, o = lax.scan(step, S0, xs)
    return o.transpose(1, 0, 3, 2, 4).reshape(bsz, T, H, Dv)


def gated_deltanet(qkv_raw, b_raw, a_raw, conv_w, a_log, dt_bias):
    bsz, T = qkv_raw.shape[0], qkv_raw.shape[1]
    qkv = jax.nn.silu(causal_depthwise_conv(qkv_raw, conv_w)).astype(jnp.float32)
    q, k, v = jnp.split(qkv, 3, axis=-1)
    q = l2_norm(q.reshape(bsz, T, C_HEADS, C_HEAD_DIM))
    k = l2_norm(k.reshape(bsz, T, C_HEADS, C_HEAD_DIM))
    v = v.reshape(bsz, T, C_HEADS, C_HEAD_DIM)
    beta = jax.nn.sigmoid(b_raw.astype(jnp.float32))
    g = -jnp.exp(a_log.astype(jnp.float32)) * jax.nn.softplus(
        a_raw.astype(jnp.float32) + dt_bias.astype(jnp.float32))
    return gated_delta_rule_chunked(q, k, v, g, beta)


def setup_inputs(seed: int = 0) -> dict:
    key = jax.random.key(seed)
    ks = jax.random.split(key, 24)
    nrm = jax.random.normal
    x = nrm(ks[0], (BATCH, SEQ, D_MODEL), jnp.float32)
    offset = jax.random.randint(ks[1], (BATCH, 1), 0, MAX_POS_OFFSET, dtype=jnp.int32)
    positions = (offset + jnp.arange(SEQ, dtype=jnp.int32)[None, :]).astype(jnp.int32)
    norm_w = 1.0 + 0.02 * nrm(ks[2], (DEPTH, D_MODEL))
    w_in = nrm(ks[3], (DEPTH, D_MODEL, IN_COLS)) * D_MODEL ** -0.5
    q_norm_w = 1.0 + 0.02 * nrm(ks[4], (DEPTH, A_HEAD_DIM))
    k_norm_w = 1.0 + 0.02 * nrm(ks[5], (DEPTH, A_HEAD_DIM))
    sinks = nrm(ks[6], (DEPTH, A_Q_HEADS))
    b_conv_w = nrm(ks[7], (DEPTH, B_CONV_WIDTH, B_WIDTH)) * B_CONV_WIDTH ** -0.5
    b_conv_b = 0.02 * nrm(ks[8], (DEPTH, B_WIDTH))
    b_ln_w = 1.0 + 0.02 * nrm(ks[9], (DEPTH, B_WIDTH))
    b_ln_b = 0.02 * nrm(ks[10], (DEPTH, B_WIDTH))
    b_pw_w = nrm(ks[11], (DEPTH, B_WIDTH, B_WIDTH)) * B_WIDTH ** -0.5
    b_pw_b = 0.02 * nrm(ks[12], (DEPTH, B_WIDTH))
    c_conv_w = nrm(ks[13], (DEPTH, C_CONV_WIDTH, 3 * C_WIDTH)) * C_CONV_WIDTH ** -0.5
    c_a_log = jnp.log(jax.random.uniform(ks[14], (DEPTH, C_HEADS), minval=1.0, maxval=16.0))
    dt = jnp.exp(jax.random.uniform(ks[15], (DEPTH, C_HEADS),
                                    minval=math.log(1e-3), maxval=math.log(1e-1)))
    c_dt_bias = dt + jnp.log(-jnp.expm1(-dt))
    c_onorm_w = 1.0 + 0.02 * nrm(ks[16], (DEPTH, C_HEAD_DIM))
    w_out = nrm(ks[17], (DEPTH, MIX_WIDTH, D_MODEL)) * MIX_WIDTH ** -0.5
    return {"x": x, "positions": positions, "norm_w": norm_w, "w_in": w_in,
            "q_norm_w": q_norm_w, "k_norm_w": k_norm_w, "sinks": sinks,
            "b_conv_w": b_conv_w, "b_conv_b": b_conv_b, "b_ln_w": b_ln_w,
            "b_ln_b": b_ln_b, "b_pw_w": b_pw_w, "b_pw_b": b_pw_b,
            "c_conv_w": c_conv_w, "c_a_log": c_a_log, "c_dt_bias": c_dt_bias,
            "c_onorm_w": c_onorm_w, "w_out": w_out}


def reference(x, positions, norm_w, w_in, q_norm_w, k_norm_w, sinks, b_conv_w, b_conv_b,
              b_ln_w, b_ln_b, b_pw_w, b_pw_b, c_conv_w, c_a_log, c_dt_bias, c_onorm_w,
              w_out):
    bsz, T = x.shape[0], x.shape[1]
    cos, sin = rope_tables(positions)
    for l in range(DEPTH):
        h = rms_norm(x, norm_w[l])
        p = h @ w_in[l]
        qa, ka, va, za, ub, zb, qkv_c, b_c, a_c, zc = _split_cols(p)
        qa = apply_partial_rope(rms_norm(qa.reshape(bsz, T, A_Q_HEADS, A_HEAD_DIM), q_norm_w[l]), cos, sin)
        ka = apply_partial_rope(rms_norm(ka.reshape(bsz, T, A_KV_HEADS, A_HEAD_DIM), k_norm_w[l]), cos, sin)
        va = va.reshape(bsz, T, A_KV_HEADS, A_HEAD_DIM)
        oa = sliding_window_sink_attention(qa, ka, va, sinks[l]) * jax.nn.silu(za)
        ob = conformer_conv_module(ub, b_conv_w[l], b_conv_b[l], b_ln_w[l], b_ln_b[l],
                                   b_pw_w[l], b_pw_b[l]) * jax.nn.silu(zb)
        oc = gated_deltanet(qkv_c, b_c, a_c, c_conv_w[l], c_a_log[l], c_dt_bias[l]).astype(x.dtype)
        oc = rms_norm(oc, c_onorm_w[l]) * jax.nn.silu(zc.reshape(bsz, T, C_HEADS, C_HEAD_DIM))
        oc = oc.reshape(bsz, T, C_WIDTH)
        y = jnp.concatenate([oa, ob, oc], axis=-1)
        x = x + y @ w_out[l]
    return x
```

```python
import functools

import jax
import jax.numpy as jnp
from jax import lax
from jax.experimental import pallas as pl
from jax.experimental.pallas import tpu as pltpu

F32 = jnp.float32
BF16 = jnp.bfloat16

EPS = 1e-6
ROPE_THETA = 500000.0
A_HEAD_DIM = 64
A_Q_HEADS = 12
A_KV_HEADS = 4
A_GROUP = 3
A_WIDTH = A_Q_HEADS * A_HEAD_DIM
A_KV_WIDTH = A_KV_HEADS * A_HEAD_DIM
ATTN_BLOCK = 128
ROT_DIM = 16
B_WIDTH = 512
B_CONV_WIDTH = 31
C_HEAD_DIM = 128
C_HEADS = 6
C_WIDTH = C_HEADS * C_HEAD_DIM
C_CONV_WIDTH = 4
CHUNK = 64
LANES = 128

COL_QKV_C = 0
COL_ZC = 3 * C_WIDTH
COL_UB = COL_ZC + C_WIDTH
COL_ZB = COL_UB + 2 * B_WIDTH
COL_QA = COL_ZB + B_WIDTH
COL_ZA = COL_QA + A_WIDTH
COL_KVA = COL_ZA + A_WIDTH
MAIN_COLS = COL_KVA + 2 * A_KV_WIDTH

NEG_BIG = -1e30
VMEM_LIMIT = 52 * 1024 * 1024


def _sigmoid(x):
    return 1.0 / (1.0 + jnp.exp(-x))


def _silu(x):
    return x * _sigmoid(x)


def _dot(a, b):
    return jnp.dot(a, b, preferred_element_type=F32)


def _dot_nt(a, b):
    return lax.dot_general(a, b, (((1,), (1,)), ((), ())), preferred_element_type=F32)


def _rope_kernel(pos_ref, invf_ref, cos_ref, sin_ref):
    p = pos_ref[...].astype(F32)
    for f in range(ROT_DIM // 2):
        ang = p * invf_ref[f:f + 1, :]
        cos_ref[f] = jnp.cos(ang)
        sin_ref[f] = jnp.sin(ang)


def _rope_tables(positions):
    bsz, seq = positions.shape
    rows = bsz * seq // LANES
    nf = ROT_DIM // 2
    inv_freq = ROPE_THETA ** (-jnp.arange(0, ROT_DIM, 2, dtype=F32) / ROT_DIM)
    invf = jnp.broadcast_to(inv_freq[:, None], (nf, LANES))
    cos, sin = pl.pallas_call(
        _rope_kernel,
        out_shape=[jax.ShapeDtypeStruct((nf, rows, LANES), F32)] * 2,
    )(positions.reshape(rows, LANES), invf)
    cos = cos.reshape(nf, bsz, seq).transpose(1, 2, 0)
    sin = sin.reshape(nf, bsz, seq).transpose(1, 2, 0)
    zeros8 = jnp.zeros_like(cos)
    rest = jnp.zeros((bsz, seq, A_HEAD_DIM - ROT_DIM), F32)
    c_tab = jnp.concatenate([cos, cos, rest + 1.0], axis=-1)
    s1_tab = jnp.concatenate([-sin, zeros8, rest], axis=-1)
    s2_tab = jnp.concatenate([zeros8, sin, rest], axis=-1)
    tile2 = lambda t: jnp.concatenate([t, t], axis=-1)
    return tile2(c_tab), tile2(s1_tab), tile2(s2_tab)


def _inproj_kernel(x_ref, nw_ref, w_ref, wba_ref, p_ref, ba_ref, h_ref, *, row_chunk):
    j = pl.program_id(1)

    @pl.when(j == 0)
    def _():
        nw = nw_ref[...]

        def body(r, carry):
            rows = pl.ds(pl.multiple_of(r * row_chunk, row_chunk), row_chunk)
            x = x_ref[rows, :]
            ms = jnp.mean(x * x, axis=-1, keepdims=True)
            h_ref[rows, :] = (x * lax.rsqrt(ms + EPS) * nw).astype(BF16)
            return carry

        lax.fori_loop(0, x_ref.shape[0] // row_chunk, body, 0)
        ba_ref[...] = _dot(h_ref[...], wba_ref[...])

    p_ref[...] = _dot(h_ref[...], w_ref[...]).astype(p_ref.dtype)


def _in_proj(x2d, norm_w, w_main, w_ba):
    n, d = x2d.shape
    cols = w_main.shape[1]
    tm = min(1024, n)
    tn = 512
    return pl.pallas_call(
        functools.partial(_inproj_kernel, row_chunk=min(128, tm)),
        grid=(n // tm, cols // tn),
        in_specs=[
            pl.BlockSpec((tm, d), lambda i, j: (i, 0)),
            pl.BlockSpec((1, d), lambda i, j: (0, 0)),
            pl.BlockSpec((d, tn), lambda i, j: (0, j)),
            pl.BlockSpec((d, LANES), lambda i, j: (0, 0)),
        ],
        out_specs=[
            pl.BlockSpec((tm, tn), lambda i, j: (i, j)),
            pl.BlockSpec((tm, LANES), lambda i, j: (i, 0)),
        ],
        out_shape=[
            jax.ShapeDtypeStruct((n, cols), BF16),
            jax.ShapeDtypeStruct((n, LANES), F32),
        ],
        scratch_shapes=[pltpu.VMEM((tm, d), BF16)],
        compiler_params=pltpu.CompilerParams(
            dimension_semantics=("parallel", "arbitrary"),
            vmem_limit_bytes=VMEM_LIMIT),
        name="in_proj",
    )(x2d, norm_w.reshape(1, d), w_main, w_ba)


def _attn_kernel(sink_ref, q_ref, za_ref, kv_ref, c_ref, s1_ref, s2_ref, qw_ref, kw_ref,
                 o_ref, kp_ref, vp_ref):
    n = pl.program_id(1)
    blk = ATTN_BLOCK
    hd = A_HEAD_DIM

    @pl.when(n == 0)
    def _():
        kp_ref[...] = jnp.zeros_like(kp_ref)
        vp_ref[...] = jnp.zeros_like(vp_ref)

    c_tab = c_ref[...]
    s1_tab = s1_ref[...]
    s2_tab = s2_ref[...]
    lane = lax.broadcasted_iota(jnp.int32, (blk, LANES), 1)
    row = lax.broadcasted_iota(jnp.int32, (blk, LANES), 0)
    lo = lane < hd

    def norm_rope(x, w):
        ss = x * x
        s_lo = jnp.sum(jnp.where(lo, ss, 0.0), axis=-1, keepdims=True)
        s_hi = jnp.sum(jnp.where(lo, 0.0, ss), axis=-1, keepdims=True)
        ms = jnp.where(lo, s_lo, s_hi) * (1.0 / hd)
        y = x * lax.rsqrt(ms + EPS) * w
        return (y * c_tab + pltpu.roll(y, LANES - ROT_DIM // 2, 1) * s1_tab
                + pltpu.roll(y, ROT_DIM // 2, 1) * s2_tab)

    kw = kw_ref[...]
    qw = qw_ref[...]
    kv = kv_ref[...]
    k_raw = kv[:, :A_KV_WIDTH].astype(F32)
    v_cur = kv[:, A_KV_WIDTH:]
    k_cur = jnp.concatenate(
        [norm_rope(k_raw[:, :LANES], kw), norm_rope(k_raw[:, LANES:], kw)],
        axis=-1).astype(BF16)
    k_prev = [kp_ref[:, g * hd:(g + 1) * hd] for g in range(A_KV_HEADS)]
    v_prev = [vp_ref[:, g * hd:(g + 1) * hd] for g in range(A_KV_HEADS)]

    thr = jnp.where(n > 0, 0, blk)
    mask_p = (lane - row) > thr
    mask_c = lane <= row

    for c in range(A_WIDTH // LANES):
        q_pair = (norm_rope(q_ref[:, c * LANES:(c + 1) * LANES].astype(F32), qw)
                  * (hd ** -0.5)).astype(BF16)
        outs = []
        for hh in range(2):
            h = 2 * c + hh
            g = h // A_GROUP
            qh = q_pair[:, hh * hd:(hh + 1) * hd]
            gs = slice(g * hd, (g + 1) * hd)
            s_c = jnp.where(mask_c, _dot_nt(qh, k_cur[:, gs]), NEG_BIG)
            s_p = jnp.where(mask_p, _dot_nt(qh, k_prev[g]), NEG_BIG)
            sink = sink_ref[h]
            m = jnp.maximum(jnp.max(s_c, axis=-1, keepdims=True),
                            jnp.max(s_p, axis=-1, keepdims=True))
            m = jnp.maximum(m, sink)
            p_c = jnp.exp(s_c - m)
            p_p = jnp.exp(s_p - m)
            den = (jnp.sum(p_c, axis=-1, keepdims=True)
                   + jnp.sum(p_p, axis=-1, keepdims=True) + jnp.exp(sink - m))
            o = _dot(p_c.astype(BF16), v_cur[:, gs]) + _dot(p_p.astype(BF16), v_prev[g])
            z = za_ref[:, h * hd:(h + 1) * hd].astype(F32)
            outs.append(o / den * _silu(z))
        o_ref[:, c * LANES:(c + 1) * LANES] = jnp.concatenate(outs, axis=-1).astype(o_ref.dtype)

    kp_ref[...] = k_cur
    vp_ref[...] = v_cur


def _attention(p3, tabs, q_norm_w, k_norm_w, sinks):
    bsz, seq, _ = p3.shape
    blk = ATTN_BLOCK
    tile2 = lambda w: jnp.concatenate([w, w]).reshape(1, LANES).astype(F32)
    tab_spec = pl.BlockSpec((None, blk, LANES), lambda b, n: (b, n, 0))
    vec_spec = pl.BlockSpec((1, LANES), lambda b, n: (0, 0))
    return pl.pallas_call(
        _attn_kernel,
        grid=(bsz, seq // blk),
        in_specs=[
            pl.BlockSpec(memory_space=pltpu.SMEM),
            pl.BlockSpec((None, blk, A_WIDTH), lambda b, n: (b, n, COL_QA // A_WIDTH)),
            pl.BlockSpec((None, blk, A_WIDTH), lambda b, n: (b, n, COL_ZA // A_WIDTH)),
            pl.BlockSpec((None, blk, 2 * A_KV_WIDTH),
                         lambda b, n: (b, n, COL_KVA // (2 * A_KV_WIDTH))),
            tab_spec, tab_spec, tab_spec, vec_spec, vec_spec,
        ],
        out_specs=pl.BlockSpec((None, blk, A_WIDTH), lambda b, n: (b, n, 0)),
        out_shape=jax.ShapeDtypeStruct((bsz, seq, A_WIDTH), BF16),
        scratch_shapes=[pltpu.VMEM((blk, A_KV_WIDTH), BF16),
                        pltpu.VMEM((blk, A_KV_WIDTH), BF16)],
        compiler_params=pltpu.CompilerParams(
            dimension_semantics=("parallel", "arbitrary")),
        name="swa_attention",
    )(sinks.astype(F32), p3, p3, p3, *tabs, tile2(q_norm_w), tile2(k_norm_w))


B_HALO = 32
B_ROW_CHUNK = 64


def _convb_kernel(ub_ref, zb_ref, cw_ref, cb_ref, lnw_ref, lnb_ref, pw_ref, pb_ref,
                  o_ref, hbuf):
    s = pl.program_id(1)
    ts = ub_ref.shape[0]

    @pl.when(s == 0)
    def _():
        hbuf[0:B_HALO, :] = jnp.zeros((B_HALO, B_WIDTH), F32)

    @pl.when(s > 0)
    def _():
        hbuf[0:B_HALO, :] = hbuf[ts:ts + B_HALO, :]

    u = ub_ref[...].astype(F32)
    hbuf[B_HALO:B_HALO + ts, :] = u[:, :B_WIDTH] * _sigmoid(u[:, B_WIDTH:])

    cb = cb_ref[...]
    lnw = lnw_ref[...]
    lnb = lnb_ref[...]
    pb = pb_ref[...]
    first = B_HALO - (B_CONV_WIDTH - 1)
    for r0 in range(0, ts, B_ROW_CHUNK):
        acc = jnp.broadcast_to(cb, (B_ROW_CHUNK, B_WIDTH))
        for k in range(B_CONV_WIDTH):
            acc = acc + cw_ref[k:k + 1, :] * hbuf[r0 + first + k:r0 + first + k + B_ROW_CHUNK, :]
        mu = jnp.mean(acc, axis=-1, keepdims=True)
        cen = acc - mu
        var = jnp.mean(cen * cen, axis=-1, keepdims=True)
        y = _silu(cen * lax.rsqrt(var + EPS) * lnw + lnb)
        out = _dot(y.astype(BF16), pw_ref[...]) + pb
        z = zb_ref[r0:r0 + B_ROW_CHUNK, :].astype(F32)
        o_ref[r0:r0 + B_ROW_CHUNK, :] = (out * _silu(z)).astype(o_ref.dtype)


def _conv_module(p3, conv_w, conv_b, ln_w, ln_b, pw_w, pw_b):
    bsz, seq, _ = p3.shape
    ts = min(256, seq)
    row = lambda v: v.reshape(1, B_WIDTH).astype(F32)
    vec_spec = pl.BlockSpec((1, B_WIDTH), lambda b, s: (0, 0))
    return pl.pallas_call(
        _convb_kernel,
        grid=(bsz, seq // ts),
        in_specs=[
            pl.BlockSpec((None, ts, 2 * B_WIDTH), lambda b, s: (b, s, COL_UB // (2 * B_WIDTH))),
            pl.BlockSpec((None, ts, B_WIDTH), lambda b, s: (b, s, COL_ZB // B_WIDTH)),
            pl.BlockSpec((B_CONV_WIDTH, B_WIDTH), lambda b, s: (0, 0)),
            vec_spec, vec_spec, vec_spec,
            pl.BlockSpec((B_WIDTH, B_WIDTH), lambda b, s: (0, 0)),
            vec_spec,
        ],
        out_specs=pl.BlockSpec((None, ts, B_WIDTH), lambda b, s: (b, s, 0)),
        out_shape=jax.ShapeDtypeStruct((bsz, seq, B_WIDTH), BF16),
        scratch_shapes=[pltpu.VMEM((ts + B_HALO, B_WIDTH), F32)],
        compiler_params=pltpu.CompilerParams(
            dimension_semantics=("parallel", "arbitrary")),
        name="conformer_conv",
    )(p3, p3, conv_w.astype(F32), row(conv_b), row(ln_w), row(ln_b),
      pw_w.astype(BF16), row(pw_b))


C_TILE = 2 * CHUNK
C_HALO = 8


def _gdn_kernel(qkv_ref, zc_ref, ba_ref, cw_ref, av_ref, dt_ref, ow_ref, o_ref, xbuf, s_ref):
    s = pl.program_id(1)
    ts = C_TILE
    dk = C_HEAD_DIM

    @pl.when(s == 0)
    def _():
        xbuf[0:C_HALO, :] = jnp.zeros((C_HALO, 3 * C_WIDTH), F32)
        s_ref[...] = jnp.zeros_like(s_ref)

    @pl.when(s > 0)
    def _():
        xbuf[0:C_HALO, :] = xbuf[ts:ts + C_HALO, :]

    xbuf[C_HALO:C_HALO + ts, :] = qkv_ref[...].astype(F32)

    def conv_silu(col0):
        cols = slice(col0, col0 + dk)
        first = C_HALO - (C_CONV_WIDTH - 1)
        acc = cw_ref[0:1, cols] * xbuf[first:first + ts, cols]
        for k in range(1, C_CONV_WIDTH):
            acc = acc + cw_ref[k:k + 1, cols] * xbuf[first + k:first + k + ts, cols]
        return _silu(acc)

    def l2n(x):
        return x * lax.rsqrt(jnp.sum(x * x, axis=-1, keepdims=True) + EPS)

    ba = ba_ref[...]
    beta = _sigmoid(ba)
    sp_in = ba + dt_ref[...]
    softplus = jnp.maximum(sp_in, 0.0) + jnp.log1p(jnp.exp(-jnp.abs(sp_in)))
    g_log = -av_ref[...] * softplus

    row128 = lax.broadcasted_iota(jnp.int32, (ts, LANES), 0)
    in_chunk = row128 & (CHUNK - 1)
    g_cum = g_log
    sh = 1
    while sh < CHUNK:
        g_cum = g_cum + jnp.where(in_chunk >= sh, pltpu.roll(g_cum, sh, 0), 0.0)
        sh *= 2
    g_cum_t = g_cum.T

    ii = lax.broadcasted_iota(jnp.int32, (CHUNK, CHUNK), 0)
    jj = lax.broadcasted_iota(jnp.int32, (CHUNK, CHUNK), 1)
    incl = ii >= jj
    strict = ii > jj
    eye = jnp.where(ii == jj, 1.0, 0.0).astype(F32)
    ow = ow_ref[...]

    for h in range(C_HEADS):
        q = l2n(conv_silu(h * dk)) * (dk ** -0.5)
        k = l2n(conv_silu(C_WIDTH + h * dk))
        v = conv_silu(2 * C_WIDTH + h * dk)
        gcol = g_cum[:, C_HEADS + h:C_HEADS + h + 1]
        bcol = beta[:, h:h + 1]

        kdec_parts = []
        for c in range(ts // CHUNK):
            rows = slice(c * CHUNK, (c + 1) * CHUNK)
            g_last = gcol[(c + 1) * CHUNK - 1:(c + 1) * CHUNK, :]
            kdec_parts.append(k[rows] * jnp.exp(g_last - gcol[rows]))
        kdec_t = jnp.concatenate(kdec_parts, axis=0).T

        state = s_ref[h]
        for c in range(ts // CHUNK):
            rows = slice(c * CHUNK, (c + 1) * CHUNK)
            qc, kc, vc = q[rows], k[rows], v[rows]
            gc = gcol[rows]
            bc = bcol[rows]
            gr = g_cum_t[C_HEADS + h:C_HEADS + h + 1, c * CHUNK:(c + 1) * CHUNK]
            decay = jnp.exp(jnp.where(incl, gc - gr, NEG_BIG))
            kb = kc * bc
            kc16 = kc.astype(BF16)
            lower = jnp.where(strict, _dot_nt(kb.astype(BF16), kc16) * decay, 0.0)
            intra = _dot_nt(qc.astype(BF16), kc16) * decay

            xp = -lower
            inv = eye + xp
            for _ in range(5):
                xp16 = xp.astype(BF16)
                xp = _dot(xp16, xp16)
                inv = inv + _dot(inv.astype(BF16), xp.astype(BF16))
            inv16 = inv.astype(BF16)
            egc = jnp.exp(gc)
            u = _dot(inv16, (vc * bc).astype(BF16))
            w = _dot(inv16, (kb * egc).astype(BF16))

            state16 = state.astype(BF16)
            v_new = u - _dot(w.astype(BF16), state16)
            v_new16 = v_new.astype(BF16)
            o = _dot((qc * egc).astype(BF16), state16) + _dot(intra.astype(BF16), v_new16)
            g_last = gc[CHUNK - 1:CHUNK, :]
            state = (state * jnp.exp(g_last)
                     + _dot(kdec_t[:, c * CHUNK:(c + 1) * CHUNK].astype(BF16), v_new16))

            on = o * lax.rsqrt(jnp.mean(o * o, axis=-1, keepdims=True) + EPS) * ow
            z = zc_ref[rows, h * dk:(h + 1) * dk].astype(F32)
            o_ref[rows, h * dk:(h + 1) * dk] = (on * _silu(z)).astype(o_ref.dtype)
        s_ref[h] = state


def _gated_deltanet(p3, ba3, conv_w, a_log, dt_bias, onorm_w):
    bsz, seq, _ = p3.shape
    ts = C_TILE
    pad = jnp.zeros((LANES - 2 * C_HEADS,), F32)
    zeros_h = jnp.zeros((C_HEADS,), F32)
    a_vec = jnp.concatenate([zeros_h, jnp.exp(a_log.astype(F32)), pad]).reshape(1, LANES)
    dt_vec = jnp.concatenate([zeros_h, dt_bias.astype(F32), pad]).reshape(1, LANES)
    vec_spec = pl.BlockSpec((1, LANES), lambda b, s: (0, 0))
    return pl.pallas_call(
        _gdn_kernel,
        grid=(bsz, seq // ts),
        in_specs=[
            pl.BlockSpec((None, ts, 3 * C_WIDTH), lambda b, s: (b, s, COL_QKV_C)),
            pl.BlockSpec((None, ts, C_WIDTH), lambda b, s: (b, s, COL_ZC // C_WIDTH)),
            pl.BlockSpec((None, ts, LANES), lambda b, s: (b, s, 0)),
            pl.BlockSpec((C_CONV_WIDTH, 3 * C_WIDTH), lambda b, s: (0, 0)),
            vec_spec, vec_spec, vec_spec,
        ],
        out_specs=pl.BlockSpec((None, ts, C_WIDTH), lambda b, s: (b, s, 0)),
        out_shape=jax.ShapeDtypeStruct((bsz, seq, C_WIDTH), BF16),
        scratch_shapes=[pltpu.VMEM((ts + C_HALO, 3 * C_WIDTH), F32),
                        pltpu.VMEM((C_HEADS, C_HEAD_DIM, C_HEAD_DIM), F32)],
        compiler_params=pltpu.CompilerParams(
            dimension_semantics=("parallel", "arbitrary")),
        name="gated_deltanet",
    )(p3, p3, ba3, conv_w.astype(F32), a_vec, dt_vec,
      onorm_w.reshape(1, LANES).astype(F32))


def _outproj_kernel(x_ref, ya_ref, yb_ref, yc_ref, wa_ref, wb_ref, wc_ref, o_ref):
    acc = _dot(ya_ref[...], wa_ref[...])
    acc = acc + _dot(yb_ref[...], wb_ref[...])
    acc = acc + _dot(yc_ref[...], wc_ref[...])
    o_ref[...] = x_ref[...] + acc


def _out_proj(x2d, ya, yb, yc, w_out):
    n, d = x2d.shape
    tm = min(512, n)
    w16 = w_out.astype(BF16)
    wa, wb, wc = w16[:A_WIDTH], w16[A_WIDTH:A_WIDTH + B_WIDTH], w16[A_WIDTH + B_WIDTH:]
    row_spec = lambda width: pl.BlockSpec((tm, width), lambda i: (i, 0))
    w_spec = lambda width: pl.BlockSpec((width, d), lambda i: (0, 0))
    return pl.pallas_call(
        _outproj_kernel,
        grid=(n // tm,),
        in_specs=[row_spec(d), row_spec(A_WIDTH), row_spec(B_WIDTH), row_spec(C_WIDTH),
                  w_spec(A_WIDTH), w_spec(B_WIDTH), w_spec(C_WIDTH)],
        out_specs=row_spec(d),
        out_shape=jax.ShapeDtypeStruct((n, d), F32),
        compiler_params=pltpu.CompilerParams(
            dimension_semantics=("parallel",),
            vmem_limit_bytes=VMEM_LIMIT),
        name="out_proj",
    )(x2d, ya, yb, yc, wa, wb, wc)


def _prep_w_in(w):
    widths = (A_WIDTH, A_KV_WIDTH, A_KV_WIDTH, A_WIDTH, 2 * B_WIDTH, B_WIDTH,
              3 * C_WIDTH, C_HEADS, C_HEADS, C_WIDTH)
    parts = []
    start = 0
    for width in widths:
        parts.append(w[:, start:start + width])
        start += width
    qa, ka, va, za, ub, zb, qkv_c, b_c, a_c, zc = parts
    main = jnp.concatenate([qkv_c, zc, ub, zb, qa, za, ka, va], axis=-1).astype(BF16)
    pad = jnp.zeros((w.shape[0], LANES - 2 * C_HEADS), w.dtype)
    ba = jnp.concatenate([b_c, a_c, pad], axis=-1).astype(BF16)
    return main, ba


def kernel(x, positions, norm_w, w_in, q_norm_w, k_norm_w, sinks, b_conv_w, b_conv_b,
           b_ln_w, b_ln_b, b_pw_w, b_pw_b, c_conv_w, c_a_log, c_dt_bias, c_onorm_w, w_out):
    bsz, seq, d = x.shape
    depth = w_in.shape[0]
    tabs = _rope_tables(positions)
    x2d = x.reshape(bsz * seq, d)
    for l in range(depth):
        w_main, w_ba = _prep_w_in(w_in[l])
        p, ba = _in_proj(x2d, norm_w[l], w_main, w_ba)
        p3 = p.reshape(bsz, seq, MAIN_COLS)
        ba3 = ba.reshape(bsz, seq, LANES)
        ya = _attention(p3, tabs, q_norm_w[l], k_norm_w[l], sinks[l])
        yb = _conv_module(p3, b_conv_w[l], b_conv_b[l], b_ln_w[l], b_ln_b[l],
                          b_pw_w[l], b_pw_b[l])
        yc = _gated_deltanet(p3, ba3, c_conv_w[l], c_a_log[l], c_dt_bias[l], c_onorm_w[l])
        x2d = _out_proj(x2d, ya.reshape(bsz * seq, A_WIDTH), yb.reshape(bsz * seq, B_WIDTH),
                        yc.reshape(bsz * seq, C_WIDTH), w_out[l])
    return x2d.reshape(bsz, seq, d)
```

```python
import functools

import jax
import jax.numpy as jnp
from jax import lax
from jax.experimental import pallas as pl
from jax.experimental.pallas import tpu as pltpu

F32 = jnp.float32
BF16 = jnp.bfloat16

EPS = 1e-6
ROPE_THETA = 500000.0
A_HEAD_DIM = 64
A_Q_HEADS = 12
A_KV_HEADS = 4
A_GROUP = 3
A_WIDTH = A_Q_HEADS * A_HEAD_DIM
A_KV_WIDTH = A_KV_HEADS * A_HEAD_DIM
ATTN_BLOCK = 128
ROT_DIM = 16
B_WIDTH = 512
B_CONV_WIDTH = 31
C_HEAD_DIM = 128
C_HEADS = 6
C_WIDTH = C_HEADS * C_HEAD_DIM
C_CONV_WIDTH = 4
CHUNK = 64
CHUNK_SHIFT = CHUNK.bit_length() - 1
LANES = 128

COL_QKV_C = 0
COL_ZC = 3 * C_WIDTH
COL_UB = COL_ZC + C_WIDTH
COL_ZB = COL_UB + 2 * B_WIDTH
COL_QA = COL_ZB + B_WIDTH
COL_ZA = COL_QA + A_WIDTH
COL_KVA = COL_ZA + A_WIDTH
MAIN_COLS = COL_KVA + 2 * A_KV_WIDTH

NEG_BIG = -1e30
VMEM_LIMIT = 52 * 1024 * 1024


def _sigmoid(x):
    return 1.0 / (1.0 + jnp.exp(-x))


def _silu(x):
    return x * _sigmoid(x)


def _dot(a, b):
    return jnp.dot(a, b, preferred_element_type=F32)


def _dot_nt(a, b):
    return lax.dot_general(a, b, (((1,), (1,)), ((), ())), preferred_element_type=F32)


def _rope_kernel(pos_ref, invf_ref, cos_ref, sin_ref):
    p = pos_ref[...].astype(F32)
    for f in range(ROT_DIM // 2):
        ang = p * invf_ref[f:f + 1, :]
        cos_ref[f] = jnp.cos(ang)
        sin_ref[f] = jnp.sin(ang)


def _rope_tables(positions):
    bsz, seq = positions.shape
    rows = bsz * seq // LANES
    nf = ROT_DIM // 2
    inv_freq = ROPE_THETA ** (-jnp.arange(0, ROT_DIM, 2, dtype=F32) / ROT_DIM)
    invf = jnp.broadcast_to(inv_freq[:, None], (nf, LANES))
    cos, sin = pl.pallas_call(
        _rope_kernel,
        out_shape=[jax.ShapeDtypeStruct((nf, rows, LANES), F32)] * 2,
    )(positions.reshape(rows, LANES), invf)
    cos = cos.reshape(nf, bsz, seq).transpose(1, 2, 0)
    sin = sin.reshape(nf, bsz, seq).transpose(1, 2, 0)
    zeros8 = jnp.zeros_like(cos)
    rest = jnp.zeros((bsz, seq, A_HEAD_DIM - ROT_DIM), F32)
    c_tab = jnp.concatenate([cos, cos, rest + 1.0], axis=-1)
    s1_tab = jnp.concatenate([-sin, zeros8, rest], axis=-1)
    s2_tab = jnp.concatenate([zeros8, sin, rest], axis=-1)
    tile2 = lambda t: jnp.concatenate([t, t], axis=-1)
    return tile2(c_tab), tile2(s1_tab), tile2(s2_tab)


def _inproj_kernel(x_ref, nw_ref, w_ref, wba_ref, p_ref, ba_ref, h_ref, *, row_chunk):
    j = pl.program_id(1)

    @pl.when(j == 0)
    def _():
        nw = nw_ref[...]

        def body(r, carry):
            rows = pl.ds(pl.multiple_of(r * row_chunk, row_chunk), row_chunk)
            x = x_ref[rows, :]
            ms = jnp.mean(x * x, axis=-1, keepdims=True)
            h_ref[rows, :] = (x * lax.rsqrt(ms + EPS) * nw).astype(BF16)
            return carry

        lax.fori_loop(0, x_ref.shape[0] // row_chunk, body, 0)
        ba_ref[...] = _dot(h_ref[...], wba_ref[...])

    p_ref[...] = _dot(h_ref[...], w_ref[...]).astype(p_ref.dtype)


def _in_proj(x2d, norm_w, w_main, w_ba):
    n, d = x2d.shape
    cols = w_main.shape[1]
    tm = min(1024, n)
    tn = 512
    return pl.pallas_call(
        functools.partial(_inproj_kernel, row_chunk=min(128, tm)),
        grid=(n // tm, cols // tn),
        in_specs=[
            pl.BlockSpec((tm, d), lambda i, j: (i, 0)),
            pl.BlockSpec((1, d), lambda i, j: (0, 0)),
            pl.BlockSpec((d, tn), lambda i, j: (0, j)),
            pl.BlockSpec((d, LANES), lambda i, j: (0, 0)),
        ],
        out_specs=[
            pl.BlockSpec((tm, tn), lambda i, j: (i, j)),
            pl.BlockSpec((tm, LANES), lambda i, j: (i, 0)),
        ],
        out_shape=[
            jax.ShapeDtypeStruct((n, cols), BF16),
            jax.ShapeDtypeStruct((n, LANES), F32),
        ],
        scratch_shapes=[pltpu.VMEM((tm, d), BF16)],
        compiler_params=pltpu.CompilerParams(
            dimension_semantics=("parallel", "arbitrary"),
            vmem_limit_bytes=VMEM_LIMIT),
        name="in_proj",
    )(x2d, norm_w.reshape(1, d), w_main, w_ba)


def _attn_kernel(sink_ref, q_ref, za_ref, kv_ref, c_ref, s1_ref, s2_ref, qw_ref, kw_ref,
                 o_ref, kp_ref, vp_ref):
    n = pl.program_id(1)
    blk = ATTN_BLOCK
    hd = A_HEAD_DIM

    @pl.when(n == 0)
    def _():
        kp_ref[...] = jnp.zeros_like(kp_ref)
        vp_ref[...] = jnp.zeros_like(vp_ref)

    c_tab = c_ref[...]
    s1_tab = s1_ref[...]
    s2_tab = s2_ref[...]
    lane = lax.broadcasted_iota(jnp.int32, (blk, LANES), 1)
    row = lax.broadcasted_iota(jnp.int32, (blk, LANES), 0)
    lo = lane < hd

    def norm_rope(x, w):
        ss = x * x
        s_lo = jnp.sum(jnp.where(lo, ss, 0.0), axis=-1, keepdims=True)
        s_hi = jnp.sum(jnp.where(lo, 0.0, ss), axis=-1, keepdims=True)
        ms = jnp.where(lo, s_lo, s_hi) * (1.0 / hd)
        y = x * lax.rsqrt(ms + EPS) * w
        return (y * c_tab + pltpu.roll(y, LANES - ROT_DIM // 2, 1) * s1_tab
                + pltpu.roll(y, ROT_DIM // 2, 1) * s2_tab)

    def both_halves(x):
        sw = pltpu.roll(x, hd, 1)
        return jnp.where(lo, x, sw), jnp.where(lo, sw, x)

    kw = kw_ref[...]
    qw = qw_ref[...]
    k_cat, v_cat = [], []
    for kc in range(A_KV_WIDTH // LANES):
        k_pair = norm_rope(kv_ref[:, kc * LANES:(kc + 1) * LANES].astype(F32), kw)
        v_pair = kv_ref[:, A_KV_WIDTH + kc * LANES:A_KV_WIDTH + (kc + 1) * LANES].astype(F32)
        for hh, (k_dup, v_dup) in enumerate(zip(both_halves(k_pair), both_halves(v_pair))):
            g = 2 * kc + hh
            k_dup = k_dup.astype(BF16)
            v_dup = v_dup.astype(BF16)
            k_cat.append(jnp.concatenate([kp_ref[g], k_dup], axis=0))
            v_cat.append(jnp.concatenate([vp_ref[g], v_dup], axis=0))
            kp_ref[g] = k_dup
            vp_ref[g] = v_dup

    q_pairs = [(norm_rope(q_ref[:, c * LANES:(c + 1) * LANES].astype(F32), qw)
                * (hd ** -0.5)).astype(BF16) for c in range(A_WIDTH // LANES)]
    zero16 = jnp.zeros((blk, LANES), BF16)

    upper = lane > row
    prev_bias = jnp.where(n > 0, 0.0, NEG_BIG)

    head_out = [None] * A_Q_HEADS
    for g in range(A_KV_HEADS):
        heads = range(g * A_GROUP, (g + 1) * A_GROUP)
        q_stack = jnp.concatenate(
            [jnp.where(lo, q_pairs[h // 2], zero16) if h % 2 == 0
             else jnp.where(lo, zero16, q_pairs[h // 2]) for h in heads], axis=0)
        s_all = _dot_nt(q_stack, k_cat[g])
        p_rows, dens = [], []
        for r, h in enumerate(heads):
            s_r = s_all[r * blk:(r + 1) * blk]
            s = jnp.where(upper, s_r[:, :blk] + prev_bias, s_r[:, blk:])
            sink = sink_ref[h]
            m = jnp.maximum(jnp.max(s, axis=-1, keepdims=True), sink)
            p = jnp.exp(s - m)
            dens.append(jnp.sum(p, axis=-1, keepdims=True) + jnp.exp(sink - m))
            p16 = p.astype(BF16)
            p_rows.append(jnp.concatenate([jnp.where(upper, p16, zero16),
                                           jnp.where(upper, zero16, p16)], axis=-1))
        o_all = _dot(jnp.concatenate(p_rows, axis=0), v_cat[g])
        for r, h in enumerate(heads):
            head_out[h] = o_all[r * blk:(r + 1) * blk] / dens[r]

    for c in range(A_WIDTH // LANES):
        z = za_ref[:, c * LANES:(c + 1) * LANES].astype(F32)
        o_pair = jnp.where(lo, head_out[2 * c], head_out[2 * c + 1])
        o_ref[:, c * LANES:(c + 1) * LANES] = (o_pair * _silu(z)).astype(o_ref.dtype)


def _attention(p3, tabs, q_norm_w, k_norm_w, sinks):
    bsz, seq, _ = p3.shape
    blk = ATTN_BLOCK
    tile2 = lambda w: jnp.concatenate([w, w]).reshape(1, LANES).astype(F32)
    tab_spec = pl.BlockSpec((None, blk, LANES), lambda b, n: (b, n, 0))
    vec_spec = pl.BlockSpec((1, LANES), lambda b, n: (0, 0))
    return pl.pallas_call(
        _attn_kernel,
        grid=(bsz, seq // blk),
        in_specs=[
            pl.BlockSpec(memory_space=pltpu.SMEM),
            pl.BlockSpec((None, blk, A_WIDTH), lambda b, n: (b, n, COL_QA // A_WIDTH)),
            pl.BlockSpec((None, blk, A_WIDTH), lambda b, n: (b, n, COL_ZA // A_WIDTH)),
            pl.BlockSpec((None, blk, 2 * A_KV_WIDTH),
                         lambda b, n: (b, n, COL_KVA // (2 * A_KV_WIDTH))),
            tab_spec, tab_spec, tab_spec, vec_spec, vec_spec,
        ],
        out_specs=pl.BlockSpec((None, blk, A_WIDTH), lambda b, n: (b, n, 0)),
        out_shape=jax.ShapeDtypeStruct((bsz, seq, A_WIDTH), BF16),
        scratch_shapes=[pltpu.VMEM((A_KV_HEADS, blk, LANES), BF16),
                        pltpu.VMEM((A_KV_HEADS, blk, LANES), BF16)],
        compiler_params=pltpu.CompilerParams(
            dimension_semantics=("parallel", "arbitrary")),
        name="swa_attention",
    )(sinks.astype(F32), p3, p3, p3, *tabs, tile2(q_norm_w), tile2(k_norm_w))


B_HALO = 32
B_ROW_CHUNK = 64


def _convb_kernel(ub_ref, zb_ref, cw_ref, cb_ref, lnw_ref, lnb_ref, pw_ref, pb_ref,
                  o_ref, hbuf):
    s = pl.program_id(1)
    ts = ub_ref.shape[0]

    @pl.when(s == 0)
    def _():
        hbuf[0:B_HALO, :] = jnp.zeros((B_HALO, B_WIDTH), F32)

    @pl.when(s > 0)
    def _():
        hbuf[0:B_HALO, :] = hbuf[ts:ts + B_HALO, :]

    u = ub_ref[...].astype(F32)
    hbuf[B_HALO:B_HALO + ts, :] = u[:, :B_WIDTH] * _sigmoid(u[:, B_WIDTH:])

    cb = cb_ref[...]
    lnw = lnw_ref[...]
    lnb = lnb_ref[...]
    pb = pb_ref[...]
    first = B_HALO - (B_CONV_WIDTH - 1)
    for r0 in range(0, ts, B_ROW_CHUNK):
        acc = jnp.broadcast_to(cb, (B_ROW_CHUNK, B_WIDTH))
        for k in range(B_CONV_WIDTH):
            acc = acc + cw_ref[k:k + 1, :] * hbuf[r0 + first + k:r0 + first + k + B_ROW_CHUNK, :]
        mu = jnp.mean(acc, axis=-1, keepdims=True)
        cen = acc - mu
        var = jnp.mean(cen * cen, axis=-1, keepdims=True)
        y = _silu(cen * lax.rsqrt(var + EPS) * lnw + lnb)
        out = _dot(y.astype(BF16), pw_ref[...]) + pb
        z = zb_ref[r0:r0 + B_ROW_CHUNK, :].astype(F32)
        o_ref[r0:r0 + B_ROW_CHUNK, :] = (out * _silu(z)).astype(o_ref.dtype)


def _conv_module(p3, conv_w, conv_b, ln_w, ln_b, pw_w, pw_b):
    bsz, seq, _ = p3.shape
    ts = min(256, seq)
    row = lambda v: v.reshape(1, B_WIDTH).astype(F32)
    vec_spec = pl.BlockSpec((1, B_WIDTH), lambda b, s: (0, 0))
    return pl.pallas_call(
        _convb_kernel,
        grid=(bsz, seq // ts),
        in_specs=[
            pl.BlockSpec((None, ts, 2 * B_WIDTH), lambda b, s: (b, s, COL_UB // (2 * B_WIDTH))),
            pl.BlockSpec((None, ts, B_WIDTH), lambda b, s: (b, s, COL_ZB // B_WIDTH)),
            pl.BlockSpec((B_CONV_WIDTH, B_WIDTH), lambda b, s: (0, 0)),
            vec_spec, vec_spec, vec_spec,
            pl.BlockSpec((B_WIDTH, B_WIDTH), lambda b, s: (0, 0)),
            vec_spec,
        ],
        out_specs=pl.BlockSpec((None, ts, B_WIDTH), lambda b, s: (b, s, 0)),
        out_shape=jax.ShapeDtypeStruct((bsz, seq, B_WIDTH), BF16),
        scratch_shapes=[pltpu.VMEM((ts + B_HALO, B_WIDTH), F32)],
        compiler_params=pltpu.CompilerParams(
            dimension_semantics=("parallel", "arbitrary")),
        name="conformer_conv",
    )(p3, p3, conv_w.astype(F32), row(conv_b), row(ln_w), row(ln_b),
      pw_w.astype(BF16), row(pw_b))


C_TILE = 2 * CHUNK
C_HALO = 8


def _gdn_kernel(qkv_ref, zc_ref, ba_ref, cw_ref, al_ref, dt_ref, ow_ref, o_ref, xbuf, s_ref):
    s = pl.program_id(1)
    ts = C_TILE
    dk = C_HEAD_DIM
    nchunk = ts // CHUNK
    heads = range(C_HEADS)

    @pl.when(s == 0)
    def _():
        xbuf[0:C_HALO, :] = jnp.zeros((C_HALO, 3 * C_WIDTH), F32)
        s_ref[...] = jnp.zeros_like(s_ref)

    @pl.when(s > 0)
    def _():
        xbuf[0:C_HALO, :] = xbuf[ts:ts + C_HALO, :]

    xbuf[C_HALO:C_HALO + ts, :] = qkv_ref[...].astype(F32)

    def conv_silu(col0):
        cols = slice(col0, col0 + dk)
        first = C_HALO - (C_CONV_WIDTH - 1)
        acc = cw_ref[0:1, cols] * xbuf[first:first + ts, cols]
        for k in range(1, C_CONV_WIDTH):
            acc = acc + cw_ref[k:k + 1, cols] * xbuf[first + k:first + k + ts, cols]
        return _silu(acc)

    def l2n(x):
        return x * lax.rsqrt(jnp.sum(x * x, axis=-1, keepdims=True) + EPS)

    lane = lax.broadcasted_iota(jnp.int32, (ts, LANES), 1)
    row = lax.broadcasted_iota(jnp.int32, (ts, LANES), 0)
    ba = ba_ref[...]
    beta = _sigmoid(ba)
    sp_in = ba + dt_ref[...]
    softplus = jnp.maximum(sp_in, 0.0) + jnp.log1p(jnp.exp(-jnp.abs(sp_in)))
    decay_lanes = (lane >= C_HEADS) & (lane < 2 * C_HEADS)
    g_log = jnp.where(decay_lanes, -jnp.exp(al_ref[...]) * softplus, 0.0)

    in_chunk = row & (CHUNK - 1)
    g_cum = g_log
    sh = 1
    while sh < CHUNK:
        g_cum = g_cum + jnp.where(in_chunk >= sh, pltpu.roll(g_cum, sh, 0), 0.0)
        sh *= 2
    g_cum_t = g_cum.T
    g_end = jnp.concatenate(
        [jnp.broadcast_to(g_cum[(c + 1) * CHUNK - 1:(c + 1) * CHUNK, :], (CHUNK, LANES))
         for c in range(nchunk)], axis=0)

    same_chunk = (row >> CHUNK_SHIFT) == (lane >> CHUNK_SHIFT)
    incl = same_chunk & (row >= lane)
    strict = same_chunk & (row > lane)
    eye = jnp.where(row == lane, 1.0, 0.0).astype(F32)
    ow = ow_ref[...]

    q, k, v, gcol, xp, inv, intra, rhs, kdec_t, lhs_s = ({} for _ in range(10))
    for h in heads:
        q[h] = l2n(conv_silu(h * dk)) * (dk ** -0.5)
        k[h] = l2n(conv_silu(C_WIDTH + h * dk))
        v[h] = conv_silu(2 * C_WIDTH + h * dk)
    for h in heads:
        gcol[h] = g_cum[:, C_HEADS + h:C_HEADS + h + 1]
        grow = g_cum_t[C_HEADS + h:C_HEADS + h + 1, :]
        bcol = beta[:, h:h + 1]
        decay = jnp.exp(jnp.where(incl, gcol[h] - grow, NEG_BIG))
        kb = k[h] * bcol
        egc = jnp.exp(gcol[h])
        k16 = k[h].astype(BF16)
        kk_qk = _dot_nt(jnp.concatenate([kb.astype(BF16), q[h].astype(BF16)], axis=0), k16)
        xp[h] = -jnp.where(strict, kk_qk[:ts] * decay, 0.0)
        intra[h] = (kk_qk[ts:] * decay).astype(BF16)
        rhs[h] = jnp.concatenate([(v[h] * bcol).astype(BF16), (kb * egc).astype(BF16)], axis=-1)
        kdec_t[h] = (k[h] * jnp.exp(g_end[:, C_HEADS + h:C_HEADS + h + 1] - gcol[h])).T.astype(BF16)
        lhs_s[h] = (q[h] * egc).astype(BF16)

    for h in heads:
        inv[h] = eye + xp[h]
        x16 = xp[h].astype(BF16)
        xp[h] = _dot(x16, x16)
    for level in range(1, 5):
        for h in heads:
            x16 = xp[h].astype(BF16)
            both = _dot(jnp.concatenate([x16, inv[h].astype(BF16)], axis=0), x16)
            xp[h] = both[:ts]
            inv[h] = inv[h] + both[ts:]
    u, w = {}, {}
    for h in heads:
        inv[h] = inv[h] + _dot(inv[h].astype(BF16), xp[h].astype(BF16))
    for h in heads:
        uw = _dot(inv[h].astype(BF16), rhs[h])
        u[h] = uw[:, :dk]
        w[h] = uw[:, dk:].astype(BF16)

    state = {h: s_ref[h] for h in heads}
    zeros_c = jnp.zeros((CHUNK, dk), BF16)
    for c in range(nchunk):
        rows = slice(c * CHUNK, (c + 1) * CHUNK)
        v_pad, o_inter = {}, {}
        for h in heads:
            ws_qs = _dot(jnp.concatenate([w[h][rows], lhs_s[h][rows]], axis=0),
                         state[h].astype(BF16))
            v_new = (u[h][rows] - ws_qs[:CHUNK]).astype(BF16)
            o_inter[h] = ws_qs[CHUNK:]
            v_pad[h] = jnp.concatenate([v_new if cc == c else zeros_c for cc in range(nchunk)], axis=0)
        for h in heads:
            av_kv = _dot(jnp.concatenate([intra[h][rows], kdec_t[h]], axis=0), v_pad[h])
            o = o_inter[h] + av_kv[:CHUNK]
            g_last = gcol[h][(c + 1) * CHUNK - 1:(c + 1) * CHUNK, :]
            state[h] = state[h] * jnp.exp(g_last) + av_kv[CHUNK:]
            on = o * lax.rsqrt(jnp.mean(o * o, axis=-1, keepdims=True) + EPS) * ow
            z = zc_ref[rows, h * dk:(h + 1) * dk].astype(F32)
            o_ref[rows, h * dk:(h + 1) * dk] = (on * _silu(z)).astype(o_ref.dtype)
    for h in heads:
        s_ref[h] = state[h]


def _gated_deltanet(p3, ba3, conv_w, a_log, dt_bias, onorm_w):
    bsz, seq, _ = p3.shape
    ts = C_TILE
    pad = jnp.zeros((LANES - 2 * C_HEADS,), F32)
    zeros_h = jnp.zeros((C_HEADS,), F32)
    al_vec = jnp.concatenate([zeros_h, a_log.astype(F32), pad]).reshape(1, LANES)
    dt_vec = jnp.concatenate([zeros_h, dt_bias.astype(F32), pad]).reshape(1, LANES)
    vec_spec = pl.BlockSpec((1, LANES), lambda b, s: (0, 0))
    return pl.pallas_call(
        _gdn_kernel,
        grid=(bsz, seq // ts),
        in_specs=[
            pl.BlockSpec((None, ts, 3 * C_WIDTH), lambda b, s: (b, s, COL_QKV_C)),
            pl.BlockSpec((None, ts, C_WIDTH), lambda b, s: (b, s, COL_ZC // C_WIDTH)),
            pl.BlockSpec((None, ts, LANES), lambda b, s: (b, s, 0)),
            pl.BlockSpec((C_CONV_WIDTH, 3 * C_WIDTH), lambda b, s: (0, 0)),
            vec_spec, vec_spec, vec_spec,
        ],
        out_specs=pl.BlockSpec((None, ts, C_WIDTH), lambda b, s: (b, s, 0)),
        out_shape=jax.ShapeDtypeStruct((bsz, seq, C_WIDTH), BF16),
        scratch_shapes=[pltpu.VMEM((ts + C_HALO, 3 * C_WIDTH), F32),
                        pltpu.VMEM((C_HEADS, C_HEAD_DIM, C_HEAD_DIM), F32)],
        compiler_params=pltpu.CompilerParams(
            dimension_semantics=("parallel", "arbitrary")),
        name="gated_deltanet",
    )(p3, p3, ba3, conv_w.astype(F32), al_vec, dt_vec,
      onorm_w.reshape(1, LANES).astype(F32))


def _outproj_kernel(x_ref, ya_ref, yb_ref, yc_ref, wa_ref, wb_ref, wc_ref, o_ref):
    acc = _dot(ya_ref[...], wa_ref[...])
    acc = acc + _dot(yb_ref[...], wb_ref[...])
    acc = acc + _dot(yc_ref[...], wc_ref[...])
    o_ref[...] = x_ref[...] + acc


def _out_proj(x2d, ya, yb, yc, w_out):
    n, d = x2d.shape
    tm = min(512, n)
    w16 = w_out.astype(BF16)
    wa, wb, wc = w16[:A_WIDTH], w16[A_WIDTH:A_WIDTH + B_WIDTH], w16[A_WIDTH + B_WIDTH:]
    row_spec = lambda width: pl.BlockSpec((tm, width), lambda i: (i, 0))
    w_spec = lambda width: pl.BlockSpec((width, d), lambda i: (0, 0))
    return pl.pallas_call(
        _outproj_kernel,
        grid=(n // tm,),
        in_specs=[row_spec(d), row_spec(A_WIDTH), row_spec(B_WIDTH), row_spec(C_WIDTH),
                  w_spec(A_WIDTH), w_spec(B_WIDTH), w_spec(C_WIDTH)],
        out_specs=row_spec(d),
        out_shape=jax.ShapeDtypeStruct((n, d), F32),
        compiler_params=pltpu.CompilerParams(
            dimension_semantics=("parallel",),
            vmem_limit_bytes=VMEM_LIMIT),
        name="out_proj",
    )(x2d, ya, yb, yc, wa, wb, wc)


_IN_WIDTHS = (A_WIDTH, A_KV_WIDTH, A_KV_WIDTH, A_WIDTH, 2 * B_WIDTH, B_WIDTH,
              3 * C_WIDTH, C_HEADS, C_HEADS, C_WIDTH)
_IN_NAMES = ("qa", "ka", "va", "za", "ub", "zb", "qkv_c", "b_c", "a_c", "zc")
_IN_START = {}
_off = 0
for _name, _width in zip(_IN_NAMES, _IN_WIDTHS):
    _IN_START[_name] = _off
    _off += _width
IN_COLS = _off
_MAIN_LAYOUT = (
    (COL_QKV_C, "qkv_c", 3 * C_WIDTH), (COL_ZC, "zc", C_WIDTH), (COL_UB, "ub", 2 * B_WIDTH),
    (COL_ZB, "zb", B_WIDTH), (COL_QA, "qa", A_WIDTH), (COL_ZA, "za", A_WIDTH),
    (COL_KVA, "ka", A_KV_WIDTH), (COL_KVA + A_KV_WIDTH, "va", A_KV_WIDTH))


def _prep_kernel(w_ref, main_ref, ba_ref):
    for dst, name, width in _MAIN_LAYOUT:
        src = _IN_START[name]
        if src % LANES == 0:
            main_ref[:, dst:dst + width] = w_ref[:, src:src + width].astype(BF16)
        else:
            base = src - src % LANES
            assert src + width == IN_COLS
            tail = w_ref[:, base:IN_COLS]
            main_ref[:, dst:dst + width] = tail[:, src - base:].astype(BF16)
    b0 = _IN_START["b_c"]
    assert b0 % LANES == 0 and b0 + LANES <= IN_COLS
    lane = lax.broadcasted_iota(jnp.int32, ba_ref.shape, 1)
    ba_ref[...] = jnp.where(lane < 2 * C_HEADS, w_ref[:, b0:b0 + LANES], 0.0).astype(BF16)


def _prep_w_in(w_in, layer):
    _, d, cols = w_in.shape
    tr = 256
    return pl.pallas_call(
        _prep_kernel,
        grid=(d // tr,),
        in_specs=[pl.BlockSpec((None, tr, cols), lambda i: (layer, i, 0))],
        out_specs=[pl.BlockSpec((tr, MAIN_COLS), lambda i: (i, 0)),
                   pl.BlockSpec((tr, LANES), lambda i: (i, 0))],
        out_shape=[jax.ShapeDtypeStruct((d, MAIN_COLS), BF16),
                   jax.ShapeDtypeStruct((d, LANES), BF16)],
        compiler_params=pltpu.CompilerParams(
            dimension_semantics=("parallel",), vmem_limit_bytes=VMEM_LIMIT),
        name="prep_w_in",
    )(w_in)


def kernel(x, positions, norm_w, w_in, q_norm_w, k_norm_w, sinks, b_conv_w, b_conv_b,
           b_ln_w, b_ln_b, b_pw_w, b_pw_b, c_conv_w, c_a_log, c_dt_bias, c_onorm_w, w_out):
    bsz, seq, d = x.shape
    depth = w_in.shape[0]
    tabs = _rope_tables(positions)
    x2d = x.reshape(bsz * seq, d)
    for l in range(depth):
        w_main, w_ba = _prep_w_in(w_in, l)
        p, ba = _in_proj(x2d, norm_w[l], w_main, w_ba)
        p3 = p.reshape(bsz, seq, MAIN_COLS)
        ba3 = ba.reshape(bsz, seq, LANES)
        ya = _attention(p3, tabs, q_norm_w[l], k_norm_w[l], sinks[l])
        yb = _conv_module(p3, b_conv_w[l], b_conv_b[l], b_ln_w[l], b_ln_b[l],
                          b_pw_w[l], b_pw_b[l])
        yc = _gated_deltanet(p3, ba3, c_conv_w[l], c_a_log[l], c_dt_bias[l], c_onorm_w[l])
        x2d = _out_proj(x2d, ya.reshape(bsz * seq, A_WIDTH), yb.reshape(bsz * seq, B_WIDTH),
                        yc.reshape(bsz * seq, C_WIDTH), w_out[l])
    return x2d.reshape(bsz, seq, d)
```

```python
import jax
import jax.numpy as jnp
from jax import lax
from jax.experimental import pallas as pl
from jax.experimental.pallas import tpu as pltpu

F32 = jnp.float32
BF16 = jnp.bfloat16

EPS = 1e-6
ROPE_THETA = 500000.0
A_HEAD_DIM = 64
A_Q_HEADS = 12
A_KV_HEADS = 4
A_GROUP = 3
A_WIDTH = A_Q_HEADS * A_HEAD_DIM
A_KV_WIDTH = A_KV_HEADS * A_HEAD_DIM
ATTN_BLOCK = 128
ROT_DIM = 16
B_WIDTH = 512
B_CONV_WIDTH = 31
C_HEAD_DIM = 128
C_HEADS = 6
C_WIDTH = C_HEADS * C_HEAD_DIM
C_CONV_WIDTH = 4
CHUNK = 64
CHUNK_SHIFT = CHUNK.bit_length() - 1
LANES = 128

COL_UB = 0
COL_ZB = 2 * B_WIDTH
COL_QA = COL_ZB + B_WIDTH
COL_ZA = COL_QA + A_WIDTH
COL_KVA = COL_ZA + A_WIDTH
AB_COLS = COL_KVA + 2 * A_KV_WIDTH
CCOL_Z = 3 * C_WIDTH
CCOL_BA = CCOL_Z + C_WIDTH
C_COLS = CCOL_BA + LANES

NEG_BIG = -1e30
VMEM_LIMIT = 56 * 1024 * 1024


def _sigmoid(x):
    return 1.0 / (1.0 + jnp.exp(-x))


def _silu(x):
    return x * _sigmoid(x)


def _dot(a, b):
    return jnp.dot(a, b, preferred_element_type=F32)


def _dot_nt(a, b):
    return lax.dot_general(a, b, (((1,), (1,)), ((), ())), preferred_element_type=F32)


def _rope_kernel(pos_ref, invf_ref, cos_ref, sin_ref):
    p = pos_ref[...].astype(F32)
    for f in range(ROT_DIM // 2):
        ang = p * invf_ref[f:f + 1, :]
        cos_ref[f] = jnp.cos(ang)
        sin_ref[f] = jnp.sin(ang)


def _rope_tables(positions):
    bsz, seq = positions.shape
    rows = bsz * seq // LANES
    nf = ROT_DIM // 2
    inv_freq = ROPE_THETA ** (-jnp.arange(0, ROT_DIM, 2, dtype=F32) / ROT_DIM)
    invf = jnp.broadcast_to(inv_freq[:, None], (nf, LANES))
    cos, sin = pl.pallas_call(
        _rope_kernel,
        out_shape=[jax.ShapeDtypeStruct((nf, rows, LANES), F32)] * 2,
    )(positions.reshape(rows, LANES), invf)
    cos = cos.reshape(nf, bsz, seq).transpose(1, 2, 0)
    sin = sin.reshape(nf, bsz, seq).transpose(1, 2, 0)
    zeros8 = jnp.zeros_like(cos)
    rest = jnp.zeros((bsz, seq, A_HEAD_DIM - ROT_DIM), F32)
    c_tab = jnp.concatenate([cos, cos, rest + 1.0], axis=-1)
    s1_tab = jnp.concatenate([-sin, zeros8, rest], axis=-1)
    s2_tab = jnp.concatenate([zeros8, sin, rest], axis=-1)
    tile2 = lambda t: jnp.concatenate([t, t], axis=-1)
    return tile2(c_tab), tile2(s1_tab), tile2(s2_tab)


def _rms_kernel(x_ref, nw_ref, h_ref):
    x = x_ref[...]
    ms = jnp.mean(x * x, axis=-1, keepdims=True)
    h_ref[...] = (x * lax.rsqrt(ms + EPS) * nw_ref[...]).astype(h_ref.dtype)


def _rms_norm(x2d, norm_w):
    n, d = x2d.shape
    tm = min(256, n)
    return pl.pallas_call(
        _rms_kernel,
        grid=(n // tm,),
        in_specs=[pl.BlockSpec((tm, d), lambda i: (i, 0)),
                  pl.BlockSpec((1, d), lambda i: (0, 0))],
        out_specs=pl.BlockSpec((tm, d), lambda i: (i, 0)),
        out_shape=jax.ShapeDtypeStruct((n, d), BF16),
        compiler_params=pltpu.CompilerParams(dimension_semantics=("parallel",)),
        name="rms_norm",
    )(x2d, norm_w.reshape(1, d).astype(F32))


def _matmul_kernel(h_ref, w_ref, p_ref):
    p_ref[...] = _dot(h_ref[...], w_ref[...]).astype(p_ref.dtype)


def _in_proj(h2d, w):
    n, d = h2d.shape
    cols = w.shape[1]
    tm = min(1024, n)
    tn = 512
    return pl.pallas_call(
        _matmul_kernel,
        grid=(n // tm, cols // tn),
        in_specs=[pl.BlockSpec((tm, d), lambda i, j: (i, 0)),
                  pl.BlockSpec((d, tn), lambda i, j: (0, j))],
        out_specs=pl.BlockSpec((tm, tn), lambda i, j: (i, j)),
        out_shape=jax.ShapeDtypeStruct((n, cols), BF16),
        compiler_params=pltpu.CompilerParams(
            dimension_semantics=("parallel", "arbitrary"),
            vmem_limit_bytes=VMEM_LIMIT),
        name="in_proj",
    )(h2d, w)


def _attn_kernel(sink_ref, q_ref, za_ref, kv_ref, c_ref, s1_ref, s2_ref, qw_ref, kw_ref,
                 o_ref, kp_ref, vp_ref):
    n = pl.program_id(1)
    blk = ATTN_BLOCK
    hd = A_HEAD_DIM

    @pl.when(n == 0)
    def _():
        kp_ref[...] = jnp.zeros_like(kp_ref)
        vp_ref[...] = jnp.zeros_like(vp_ref)

    c_tab = c_ref[...]
    s1_tab = s1_ref[...]
    s2_tab = s2_ref[...]
    lane = lax.broadcasted_iota(jnp.int32, (blk, LANES), 1)
    row = lax.broadcasted_iota(jnp.int32, (blk, LANES), 0)
    lo = lane < hd

    def norm_rope(x, w):
        ss = x * x
        s_lo = jnp.sum(jnp.where(lo, ss, 0.0), axis=-1, keepdims=True)
        s_hi = jnp.sum(jnp.where(lo, 0.0, ss), axis=-1, keepdims=True)
        ms = jnp.where(lo, s_lo, s_hi) * (1.0 / hd)
        y = x * lax.rsqrt(ms + EPS) * w
        return (y * c_tab + pltpu.roll(y, LANES - ROT_DIM // 2, 1) * s1_tab
                + pltpu.roll(y, ROT_DIM // 2, 1) * s2_tab)

    def both_halves(x):
        sw = pltpu.roll(x, hd, 1)
        return jnp.where(lo, x, sw), jnp.where(lo, sw, x)

    kw = kw_ref[...]
    qw = qw_ref[...]
    k_cat, v_cat = [], []
    for kc in range(A_KV_WIDTH // LANES):
        k_pair = norm_rope(kv_ref[:, kc * LANES:(kc + 1) * LANES].astype(F32), kw)
        v_pair = kv_ref[:, A_KV_WIDTH + kc * LANES:A_KV_WIDTH + (kc + 1) * LANES].astype(F32)
        for hh, (k_dup, v_dup) in enumerate(zip(both_halves(k_pair), both_halves(v_pair))):
            g = 2 * kc + hh
            k_dup = k_dup.astype(BF16)
            v_dup = v_dup.astype(BF16)
            k_cat.append(jnp.concatenate([kp_ref[g], k_dup], axis=0))
            v_cat.append(jnp.concatenate([vp_ref[g], v_dup], axis=0))
            kp_ref[g] = k_dup
            vp_ref[g] = v_dup

    q_pairs = [(norm_rope(q_ref[:, c * LANES:(c + 1) * LANES].astype(F32), qw)
                * (hd ** -0.5)).astype(BF16) for c in range(A_WIDTH // LANES)]
    zero16 = jnp.zeros((blk, LANES), BF16)

    upper = lane > row
    prev_bias = jnp.where(n > 0, 0.0, NEG_BIG)

    head_out = [None] * A_Q_HEADS
    for g in range(A_KV_HEADS):
        heads = range(g * A_GROUP, (g + 1) * A_GROUP)
        q_stack = jnp.concatenate(
            [jnp.where(lo, q_pairs[h // 2], zero16) if h % 2 == 0
             else jnp.where(lo, zero16, q_pairs[h // 2]) for h in heads], axis=0)
        s_all = _dot_nt(q_stack, k_cat[g])
        p_rows, dens = [], []
        for r, h in enumerate(heads):
            s_r = s_all[r * blk:(r + 1) * blk]
            s = jnp.where(upper, s_r[:, :blk] + prev_bias, s_r[:, blk:])
            sink = sink_ref[h]
            m = jnp.maximum(jnp.max(s, axis=-1, keepdims=True), sink)
            p = jnp.exp(s - m)
            dens.append(jnp.sum(p, axis=-1, keepdims=True) + jnp.exp(sink - m))
            p16 = p.astype(BF16)
            p_rows.append(jnp.concatenate([jnp.where(upper, p16, zero16),
                                           jnp.where(upper, zero16, p16)], axis=-1))
        o_all = _dot(jnp.concatenate(p_rows, axis=0), v_cat[g])
        for r, h in enumerate(heads):
            head_out[h] = o_all[r * blk:(r + 1) * blk] / dens[r]

    for c in range(A_WIDTH // LANES):
        z = za_ref[:, c * LANES:(c + 1) * LANES].astype(F32)
        o_pair = jnp.where(lo, head_out[2 * c], head_out[2 * c + 1])
        o_ref[:, c * LANES:(c + 1) * LANES] = (o_pair * _silu(z)).astype(o_ref.dtype)


def _attention(p3, tabs, q_norm_w, k_norm_w, sinks):
    bsz, seq, _ = p3.shape
    blk = ATTN_BLOCK
    tile2 = lambda w: jnp.concatenate([w, w]).reshape(1, LANES).astype(F32)
    tab_spec = pl.BlockSpec((None, blk, LANES), lambda b, n: (b, n, 0))
    vec_spec = pl.BlockSpec((1, LANES), lambda b, n: (0, 0))
    return pl.pallas_call(
        _attn_kernel,
        grid=(bsz, seq // blk),
        in_specs=[
            pl.BlockSpec(memory_space=pltpu.SMEM),
            pl.BlockSpec((None, blk, A_WIDTH), lambda b, n: (b, n, COL_QA // A_WIDTH)),
            pl.BlockSpec((None, blk, A_WIDTH), lambda b, n: (b, n, COL_ZA // A_WIDTH)),
            pl.BlockSpec((None, blk, 2 * A_KV_WIDTH),
                         lambda b, n: (b, n, COL_KVA // (2 * A_KV_WIDTH))),
            tab_spec, tab_spec, tab_spec, vec_spec, vec_spec,
        ],
        out_specs=pl.BlockSpec((None, blk, A_WIDTH), lambda b, n: (b, n, 0)),
        out_shape=jax.ShapeDtypeStruct((bsz, seq, A_WIDTH), BF16),
        scratch_shapes=[pltpu.VMEM((A_KV_HEADS, blk, LANES), BF16),
                        pltpu.VMEM((A_KV_HEADS, blk, LANES), BF16)],
        compiler_params=pltpu.CompilerParams(
            dimension_semantics=("parallel", "arbitrary")),
        name="swa_attention",
    )(sinks.astype(F32), p3, p3, p3, *tabs, tile2(q_norm_w), tile2(k_norm_w))


B_HALO = 32
B_ROW_CHUNK = 64


def _convb_kernel(ub_ref, zb_ref, cw_ref, cb_ref, lnw_ref, lnb_ref, pw_ref, pb_ref,
                  o_ref, hbuf):
    s = pl.program_id(1)
    ts = ub_ref.shape[0]

    @pl.when(s == 0)
    def _():
        hbuf[0:B_HALO, :] = jnp.zeros((B_HALO, B_WIDTH), F32)

    @pl.when(s > 0)
    def _():
        hbuf[0:B_HALO, :] = hbuf[ts:ts + B_HALO, :]

    u = ub_ref[...].astype(F32)
    hbuf[B_HALO:B_HALO + ts, :] = u[:, :B_WIDTH] * _sigmoid(u[:, B_WIDTH:])

    cb = cb_ref[...]
    lnw = lnw_ref[...]
    lnb = lnb_ref[...]
    pb = pb_ref[...]
    first = B_HALO - (B_CONV_WIDTH - 1)
    for r0 in range(0, ts, B_ROW_CHUNK):
        acc = jnp.broadcast_to(cb, (B_ROW_CHUNK, B_WIDTH))
        for k in range(B_CONV_WIDTH):
            acc = acc + cw_ref[k:k + 1, :] * hbuf[r0 + first + k:r0 + first + k + B_ROW_CHUNK, :]
        mu = jnp.mean(acc, axis=-1, keepdims=True)
        cen = acc - mu
        var = jnp.mean(cen * cen, axis=-1, keepdims=True)
        y = _silu(cen * lax.rsqrt(var + EPS) * lnw + lnb)
        out = _dot(y.astype(BF16), pw_ref[...]) + pb
        z = zb_ref[r0:r0 + B_ROW_CHUNK, :].astype(F32)
        o_ref[r0:r0 + B_ROW_CHUNK, :] = (out * _silu(z)).astype(o_ref.dtype)


def _conv_module(p3, conv_w, conv_b, ln_w, ln_b, pw_w, pw_b):
    bsz, seq, _ = p3.shape
    ts = min(256, seq)
    row = lambda v: v.reshape(1, B_WIDTH).astype(F32)
    vec_spec = pl.BlockSpec((1, B_WIDTH), lambda b, s: (0, 0))
    return pl.pallas_call(
        _convb_kernel,
        grid=(bsz, seq // ts),
        in_specs=[
            pl.BlockSpec((None, ts, 2 * B_WIDTH), lambda b, s: (b, s, COL_UB // (2 * B_WIDTH))),
            pl.BlockSpec((None, ts, B_WIDTH), lambda b, s: (b, s, COL_ZB // B_WIDTH)),
            pl.BlockSpec((B_CONV_WIDTH, B_WIDTH), lambda b, s: (0, 0)),
            vec_spec, vec_spec, vec_spec,
            pl.BlockSpec((B_WIDTH, B_WIDTH), lambda b, s: (0, 0)),
            vec_spec,
        ],
        out_specs=pl.BlockSpec((None, ts, B_WIDTH), lambda b, s: (b, s, 0)),
        out_shape=jax.ShapeDtypeStruct((bsz, seq, B_WIDTH), BF16),
        scratch_shapes=[pltpu.VMEM((ts + B_HALO, B_WIDTH), F32)],
        compiler_params=pltpu.CompilerParams(
            dimension_semantics=("parallel", "arbitrary")),
        name="conformer_conv",
    )(p3, p3, conv_w.astype(F32), row(conv_b), row(ln_w), row(ln_b),
      pw_w.astype(BF16), row(pw_b))


C_SUB = 2 * CHUNK
C_PROJ_TILE = 256
C_HALO = 8


def _gdn_mix(xb, zb, r0, out_row, cw_ref, al_ref, dt_ref, ow_ref, o_ref, s_ref):
    ts = C_SUB
    dk = C_HEAD_DIM
    nchunk = ts // CHUNK
    heads = range(C_HEADS)

    def conv_silu(col0):
        cols = slice(col0, col0 + dk)
        first = C_HALO + r0 - (C_CONV_WIDTH - 1)
        acc = cw_ref[0:1, cols] * xb[first:first + ts, cols]
        for k in range(1, C_CONV_WIDTH):
            acc = acc + cw_ref[k:k + 1, cols] * xb[first + k:first + k + ts, cols]
        return _silu(acc)

    def l2n(x):
        return x * lax.rsqrt(jnp.sum(x * x, axis=-1, keepdims=True) + EPS)

    lane = lax.broadcasted_iota(jnp.int32, (ts, LANES), 1)
    row = lax.broadcasted_iota(jnp.int32, (ts, LANES), 0)
    ba = zb[r0:r0 + ts, C_WIDTH:C_WIDTH + LANES]
    beta = _sigmoid(ba)
    sp_in = ba + dt_ref[...]
    softplus = jnp.maximum(sp_in, 0.0) + jnp.log1p(jnp.exp(-jnp.abs(sp_in)))
    decay_lanes = (lane >= C_HEADS) & (lane < 2 * C_HEADS)
    g_log = jnp.where(decay_lanes, -jnp.exp(al_ref[...]) * softplus, 0.0)

    in_chunk = row & (CHUNK - 1)
    g_cum = g_log
    sh = 1
    while sh < CHUNK:
        g_cum = g_cum + jnp.where(in_chunk >= sh, pltpu.roll(g_cum, sh, 0), 0.0)
        sh *= 2
    g_cum_t = g_cum.T
    g_end = jnp.concatenate(
        [jnp.broadcast_to(g_cum[(c + 1) * CHUNK - 1:(c + 1) * CHUNK, :], (CHUNK, LANES))
         for c in range(nchunk)], axis=0)

    same_chunk = (row >> CHUNK_SHIFT) == (lane >> CHUNK_SHIFT)
    incl = same_chunk & (row >= lane)
    strict = same_chunk & (row > lane)
    eye = jnp.where(row == lane, 1.0, 0.0).astype(F32)
    ow = ow_ref[...]

    q, k, v, gcol, xp, inv, intra, rhs, kdec_t, lhs_s = ({} for _ in range(10))
    for h in heads:
        q[h] = l2n(conv_silu(h * dk)) * (dk ** -0.5)
        k[h] = l2n(conv_silu(C_WIDTH + h * dk))
        v[h] = conv_silu(2 * C_WIDTH + h * dk)
    for h in heads:
        gcol[h] = g_cum[:, C_HEADS + h:C_HEADS + h + 1]
        grow = g_cum_t[C_HEADS + h:C_HEADS + h + 1, :]
        bcol = beta[:, h:h + 1]
        decay = jnp.exp(jnp.where(incl, gcol[h] - grow, NEG_BIG))
        kb = k[h] * bcol
        egc = jnp.exp(gcol[h])
        k16 = k[h].astype(BF16)
        kk_qk = _dot_nt(jnp.concatenate([kb.astype(BF16), q[h].astype(BF16)], axis=0), k16)
        xp[h] = -jnp.where(strict, kk_qk[:ts] * decay, 0.0)
        intra[h] = (kk_qk[ts:] * decay).astype(BF16)
        rhs[h] = jnp.concatenate([(v[h] * bcol).astype(BF16), (kb * egc).astype(BF16)], axis=-1)
        kdec_t[h] = (k[h] * jnp.exp(g_end[:, C_HEADS + h:C_HEADS + h + 1] - gcol[h])).T.astype(BF16)
        lhs_s[h] = (q[h] * egc).astype(BF16)

    for h in heads:
        inv[h] = eye + xp[h]
        x16 = xp[h].astype(BF16)
        xp[h] = _dot(x16, x16)
    for level in range(1, 5):
        for h in heads:
            x16 = xp[h].astype(BF16)
            both = _dot(jnp.concatenate([x16, inv[h].astype(BF16)], axis=0), x16)
            xp[h] = both[:ts]
            inv[h] = inv[h] + both[ts:]
    u, w = {}, {}
    for h in heads:
        inv[h] = inv[h] + _dot(inv[h].astype(BF16), xp[h].astype(BF16))
    for h in heads:
        uw = _dot(inv[h].astype(BF16), rhs[h])
        u[h] = uw[:, :dk]
        w[h] = uw[:, dk:].astype(BF16)

    state = {h: s_ref[h] for h in heads}
    zeros_c = jnp.zeros((CHUNK, dk), BF16)
    for c in range(nchunk):
        rows = slice(c * CHUNK, (c + 1) * CHUNK)
        v_pad, o_inter = {}, {}
        for h in heads:
            ws_qs = _dot(jnp.concatenate([w[h][rows], lhs_s[h][rows]], axis=0),
                         state[h].astype(BF16))
            v_new = (u[h][rows] - ws_qs[:CHUNK]).astype(BF16)
            o_inter[h] = ws_qs[CHUNK:]
            v_pad[h] = jnp.concatenate([v_new if cc == c else zeros_c for cc in range(nchunk)], axis=0)
        for h in heads:
            av_kv = _dot(jnp.concatenate([intra[h][rows], kdec_t[h]], axis=0), v_pad[h])
            o = o_inter[h] + av_kv[:CHUNK]
            g_last = gcol[h][(c + 1) * CHUNK - 1:(c + 1) * CHUNK, :]
            state[h] = state[h] * jnp.exp(g_last) + av_kv[CHUNK:]
            on = o * lax.rsqrt(jnp.mean(o * o, axis=-1, keepdims=True) + EPS) * ow
            z = zb[r0 + c * CHUNK:r0 + (c + 1) * CHUNK, h * dk:(h + 1) * dk]
            o_ref[pl.ds(out_row + c * CHUNK, CHUNK), h * dk:(h + 1) * dk] = (
                (on * _silu(z)).astype(o_ref.dtype))
    for h in heads:
        s_ref[h] = state[h]


def _gdn_kernel(h_ref, w_ref, cw_ref, al_ref, dt_ref, ow_ref, o_ref, xb0, xb1, zb0, zb1, s_ref):
    pt = C_PROJ_TILE
    ntile = h_ref.shape[0] // pt
    assert ntile % 2 == 0

    def project(tile, xb, zb, xb_prev):
        rows = pl.ds(pl.multiple_of(tile * pt, pt), pt)
        hh = h_ref[rows, :]
        xb[C_HALO:C_HALO + pt, :] = _dot(hh, w_ref[:, :CCOL_Z])
        zb[...] = _dot(hh, w_ref[:, CCOL_Z:])
        if xb_prev is None:
            xb[0:C_HALO, :] = jnp.zeros((C_HALO, CCOL_Z), F32)
        else:
            xb[0:C_HALO, :] = xb_prev[pt:pt + C_HALO, :]

    def mix(tile, xb, zb):
        for r0 in range(0, pt, C_SUB):
            out_row = pl.multiple_of(tile * pt + r0, C_SUB)
            _gdn_mix(xb, zb, r0, out_row, cw_ref, al_ref, dt_ref, ow_ref, o_ref, s_ref)

    s_ref[...] = jnp.zeros_like(s_ref)
    project(0, xb0, zb0, None)

    def pair(j, carry):
        project(2 * j + 1, xb1, zb1, xb0)
        mix(2 * j, xb0, zb0)
        project(jnp.minimum(2 * j + 2, ntile - 1), xb0, zb0, xb1)
        mix(2 * j + 1, xb1, zb1)
        return carry

    lax.fori_loop(0, ntile // 2, pair, 0)


def _gated_deltanet(h3, w_c, conv_w, a_log, dt_bias, onorm_w):
    bsz, seq, d = h3.shape
    pt = C_PROJ_TILE
    pad = jnp.zeros((LANES - 2 * C_HEADS,), F32)
    zeros_h = jnp.zeros((C_HEADS,), F32)
    al_vec = jnp.concatenate([zeros_h, a_log.astype(F32), pad]).reshape(1, LANES)
    dt_vec = jnp.concatenate([zeros_h, dt_bias.astype(F32), pad]).reshape(1, LANES)
    vec_spec = pl.BlockSpec((1, LANES), lambda b: (0, 0))
    return pl.pallas_call(
        _gdn_kernel,
        grid=(bsz,),
        in_specs=[
            pl.BlockSpec((None, seq, d), lambda b: (b, 0, 0)),
            pl.BlockSpec((d, C_COLS), lambda b: (0, 0), pipeline_mode=pl.Buffered(1)),
            pl.BlockSpec((C_CONV_WIDTH, 3 * C_WIDTH), lambda b: (0, 0)),
            vec_spec, vec_spec, vec_spec,
        ],
        out_specs=pl.BlockSpec((None, seq, C_WIDTH), lambda b: (b, 0, 0)),
        out_shape=jax.ShapeDtypeStruct((bsz, seq, C_WIDTH), BF16),
        scratch_shapes=[pltpu.VMEM((C_HALO + pt, 3 * C_WIDTH), F32),
                        pltpu.VMEM((C_HALO + pt, 3 * C_WIDTH), F32),
                        pltpu.VMEM((pt, C_WIDTH + LANES), F32),
                        pltpu.VMEM((pt, C_WIDTH + LANES), F32),
                        pltpu.VMEM((C_HEADS, C_HEAD_DIM, C_HEAD_DIM), F32)],
        compiler_params=pltpu.CompilerParams(
            dimension_semantics=("parallel",), vmem_limit_bytes=VMEM_LIMIT),
        name="gated_deltanet",
    )(h3, w_c, conv_w.astype(F32), al_vec, dt_vec, onorm_w.reshape(1, LANES).astype(F32))


def _outproj_kernel(x_ref, ya_ref, yb_ref, yc_ref, wa_ref, wb_ref, wc_ref, o_ref):
    acc = _dot(ya_ref[...], wa_ref[...])
    acc = acc + _dot(yb_ref[...], wb_ref[...])
    acc = acc + _dot(yc_ref[...], wc_ref[...])
    o_ref[...] = x_ref[...] + acc


def _out_proj(x2d, ya, yb, yc, w_out):
    n, d = x2d.shape
    tm = min(512, n)
    w16 = w_out.astype(BF16)
    wa, wb, wc = w16[:A_WIDTH], w16[A_WIDTH:A_WIDTH + B_WIDTH], w16[A_WIDTH + B_WIDTH:]
    row_spec = lambda width: pl.BlockSpec((tm, width), lambda i: (i, 0))
    w_spec = lambda width: pl.BlockSpec((width, d), lambda i: (0, 0))
    return pl.pallas_call(
        _outproj_kernel,
        grid=(n // tm,),
        in_specs=[row_spec(d), row_spec(A_WIDTH), row_spec(B_WIDTH), row_spec(C_WIDTH),
                  w_spec(A_WIDTH), w_spec(B_WIDTH), w_spec(C_WIDTH)],
        out_specs=row_spec(d),
        out_shape=jax.ShapeDtypeStruct((n, d), F32),
        compiler_params=pltpu.CompilerParams(
            dimension_semantics=("parallel",),
            vmem_limit_bytes=VMEM_LIMIT),
        name="out_proj",
    )(x2d, ya, yb, yc, wa, wb, wc)


_IN_WIDTHS = (A_WIDTH, A_KV_WIDTH, A_KV_WIDTH, A_WIDTH, 2 * B_WIDTH, B_WIDTH,
              3 * C_WIDTH, C_HEADS, C_HEADS, C_WIDTH)
_IN_NAMES = ("qa", "ka", "va", "za", "ub", "zb", "qkv_c", "b_c", "a_c", "zc")
_IN_START = {}
_off = 0
for _name, _width in zip(_IN_NAMES, _IN_WIDTHS):
    _IN_START[_name] = _off
    _off += _width
IN_COLS = _off
_AB_LAYOUT = (
    (COL_UB, "ub", 2 * B_WIDTH), (COL_ZB, "zb", B_WIDTH), (COL_QA, "qa", A_WIDTH),
    (COL_ZA, "za", A_WIDTH), (COL_KVA, "ka", A_KV_WIDTH), (COL_KVA + A_KV_WIDTH, "va", A_KV_WIDTH))
_C_LAYOUT = ((0, "qkv_c", 3 * C_WIDTH), (CCOL_Z, "zc", C_WIDTH))


def _prep_kernel(w_ref, ab_ref, c_ref):
    for out_ref, layout in ((ab_ref, _AB_LAYOUT), (c_ref, _C_LAYOUT)):
        for dst, name, width in layout:
            src = _IN_START[name]
            if src % LANES == 0:
                out_ref[:, dst:dst + width] = w_ref[:, src:src + width].astype(BF16)
            else:
                base = src - src % LANES
                assert src + width == IN_COLS
                tail = w_ref[:, base:IN_COLS]
                out_ref[:, dst:dst + width] = tail[:, src - base:].astype(BF16)
    b0 = _IN_START["b_c"]
    assert b0 % LANES == 0 and b0 + LANES <= IN_COLS
    lane = lax.broadcasted_iota(jnp.int32, (w_ref.shape[0], LANES), 1)
    c_ref[:, CCOL_BA:CCOL_BA + LANES] = jnp.where(
        lane < 2 * C_HEADS, w_ref[:, b0:b0 + LANES], 0.0).astype(BF16)


def _prep_w_in(w_in, layer):
    _, d, cols = w_in.shape
    tr = 256
    return pl.pallas_call(
        _prep_kernel,
        grid=(d // tr,),
        in_specs=[pl.BlockSpec((None, tr, cols), lambda i: (layer, i, 0))],
        out_specs=[pl.BlockSpec((tr, AB_COLS), lambda i: (i, 0)),
                   pl.BlockSpec((tr, C_COLS), lambda i: (i, 0))],
        out_shape=[jax.ShapeDtypeStruct((d, AB_COLS), BF16),
                   jax.ShapeDtypeStruct((d, C_COLS), BF16)],
        compiler_params=pltpu.CompilerParams(
            dimension_semantics=("parallel",), vmem_limit_bytes=VMEM_LIMIT),
        name="prep_w_in",
    )(w_in)


def _layer_parts(l, x2d, bsz, seq, prm, tabs=None):
    d = x2d.shape[1]
    if tabs is None:
        tabs = _rope_tables(prm["positions"])
    w_ab, w_c = _prep_w_in(prm["w_in"], l)
    h2d = _rms_norm(x2d, prm["norm_w"][l])
    p3 = _in_proj(h2d, w_ab).reshape(bsz, seq, AB_COLS)
    ya = _attention(p3, tabs, prm["q_norm_w"][l], prm["k_norm_w"][l], prm["sinks"][l])
    yb = _conv_module(p3, prm["b_conv_w"][l], prm["b_conv_b"][l], prm["b_ln_w"][l],
                      prm["b_ln_b"][l], prm["b_pw_w"][l], prm["b_pw_b"][l])
    yc = _gated_deltanet(h2d.reshape(bsz, seq, d), w_c, prm["c_conv_w"][l], prm["c_a_log"][l],
                         prm["c_dt_bias"][l], prm["c_onorm_w"][l])
    xo = _out_proj(x2d, ya.reshape(bsz * seq, A_WIDTH), yb.reshape(bsz * seq, B_WIDTH),
                   yc.reshape(bsz * seq, C_WIDTH), prm["w_out"][l])
    return {"ya": ya, "yb": yb, "yc": yc, "xo": xo}


def kernel(x, positions, norm_w, w_in, q_norm_w, k_norm_w, sinks, b_conv_w, b_conv_b,
           b_ln_w, b_ln_b, b_pw_w, b_pw_b, c_conv_w, c_a_log, c_dt_bias, c_onorm_w, w_out):
    bsz, seq, d = x.shape
    prm = dict(positions=positions, norm_w=norm_w, w_in=w_in, q_norm_w=q_norm_w,
               k_norm_w=k_norm_w, sinks=sinks, b_conv_w=b_conv_w, b_conv_b=b_conv_b,
               b_ln_w=b_ln_w, b_ln_b=b_ln_b, b_pw_w=b_pw_w, b_pw_b=b_pw_b, c_conv_w=c_conv_w,
               c_a_log=c_a_log, c_dt_bias=c_dt_bias, c_onorm_w=c_onorm_w, w_out=w_out)
    tabs = _rope_tables(positions)
    x2d = x.reshape(bsz * seq, d)
    for l in range(w_in.shape[0]):
        x2d = _layer_parts(l, x2d, bsz, seq, prm, tabs)["xo"]
    return x2d.reshape(bsz, seq, d)
```

```python
import jax
import jax.numpy as jnp
from jax import lax
from jax.experimental import pallas as pl
from jax.experimental.pallas import tpu as pltpu

F32 = jnp.float32
BF16 = jnp.bfloat16

EPS = 1e-6
ROPE_THETA = 500000.0
A_HEAD_DIM = 64
A_Q_HEADS = 12
A_KV_HEADS = 4
A_GROUP = 3
A_WIDTH = A_Q_HEADS * A_HEAD_DIM
A_KV_WIDTH = A_KV_HEADS * A_HEAD_DIM
ATTN_BLOCK = 128
ROT_DIM = 16
B_WIDTH = 512
B_CONV_WIDTH = 31
C_HEAD_DIM = 128
C_HEADS = 6
C_WIDTH = C_HEADS * C_HEAD_DIM
C_CONV_WIDTH = 4
CHUNK = 64
CHUNK_SHIFT = CHUNK.bit_length() - 1
LANES = 128
SUBLANES = 8

COL_UB = 0
COL_ZB = 2 * B_WIDTH
COL_QA = COL_ZB + B_WIDTH
COL_ZA = COL_QA + A_WIDTH
COL_KVA = COL_ZA + A_WIDTH
AB_COLS = COL_KVA + 2 * A_KV_WIDTH
CCOL_Z = 3 * C_WIDTH
CCOL_BA = CCOL_Z + C_WIDTH
C_COLS = CCOL_BA + LANES

NEG_BIG = -1e30
VMEM_LIMIT = 56 * 1024 * 1024


def _sigmoid(x):
    return 1.0 / (1.0 + jnp.exp(-x))


def _silu(x):
    return x * _sigmoid(x)


def _dot(a, b):
    return jnp.dot(a, b, preferred_element_type=F32)


def _dot_nt(a, b):
    return lax.dot_general(a, b, (((1,), (1,)), ((), ())), preferred_element_type=F32)


def _rope_kernel(pos_ref, invf_ref, cos_ref, sin_ref):
    p = pos_ref[...].astype(F32)
    for f in range(ROT_DIM // 2):
        ang = p * invf_ref[f:f + 1, :]
        cos_ref[f] = jnp.cos(ang)
        sin_ref[f] = jnp.sin(ang)


def _rope_tables(positions):
    bsz, seq = positions.shape
    rows = bsz * seq // LANES
    nf = ROT_DIM // 2
    inv_freq = ROPE_THETA ** (-jnp.arange(0, ROT_DIM, 2, dtype=F32) / ROT_DIM)
    invf = jnp.broadcast_to(inv_freq[:, None], (nf, LANES))
    cos, sin = pl.pallas_call(
        _rope_kernel,
        out_shape=[jax.ShapeDtypeStruct((nf, rows, LANES), F32)] * 2,
    )(positions.reshape(rows, LANES), invf)
    cos = cos.reshape(nf, bsz, seq).transpose(1, 2, 0)
    sin = sin.reshape(nf, bsz, seq).transpose(1, 2, 0)
    zeros8 = jnp.zeros_like(cos)
    rest = jnp.zeros((bsz, seq, A_HEAD_DIM - ROT_DIM), F32)
    c_tab = jnp.concatenate([cos, cos, rest + 1.0], axis=-1)
    s1_tab = jnp.concatenate([-sin, zeros8, rest], axis=-1)
    s2_tab = jnp.concatenate([zeros8, sin, rest], axis=-1)
    tile2 = lambda t: jnp.concatenate([t, t], axis=-1)
    return tile2(c_tab), tile2(s1_tab), tile2(s2_tab)


NORM_ROW_CHUNK = 128


def _inproj_kernel(x_ref, nw_ref, w_ref, p_ref, h_ref):
    j = pl.program_id(1)

    @pl.when(j == 0)
    def _():
        nw = nw_ref[...]
        chunk = min(NORM_ROW_CHUNK, x_ref.shape[0])

        def body(r, carry):
            rows = pl.ds(pl.multiple_of(r * chunk, chunk), chunk)
            x = x_ref[rows, :]
            ms = jnp.mean(x * x, axis=-1, keepdims=True)
            h_ref[rows, :] = (x * lax.rsqrt(ms + EPS) * nw).astype(h_ref.dtype)
            return carry

        lax.fori_loop(0, x_ref.shape[0] // chunk, body, 0)

    p_ref[...] = _dot(h_ref[...], w_ref[...]).astype(p_ref.dtype)


def _in_proj(x2d, norm_w, w):
    n, d = x2d.shape
    cols = w.shape[1]
    tm = min(1024, n)
    tn = 512
    return pl.pallas_call(
        _inproj_kernel,
        grid=(n // tm, cols // tn),
        in_specs=[pl.BlockSpec((tm, d), lambda i, j: (i, 0)),
                  pl.BlockSpec((1, d), lambda i, j: (0, 0)),
                  pl.BlockSpec((d, tn), lambda i, j: (0, j))],
        out_specs=[pl.BlockSpec((tm, tn), lambda i, j: (i, j)),
                   pl.BlockSpec((tm, d), lambda i, j: (i, 0))],
        out_shape=[jax.ShapeDtypeStruct((n, cols), BF16),
                   jax.ShapeDtypeStruct((n, d), BF16)],
        compiler_params=pltpu.CompilerParams(
            dimension_semantics=("parallel", "arbitrary"),
            vmem_limit_bytes=VMEM_LIMIT),
        name="in_proj",
    )(x2d, norm_w.reshape(1, d).astype(F32), w)


def _attn_kernel(sink_ref, q_ref, za_ref, kv_ref, c_ref, s1_ref, s2_ref, qw_ref, kw_ref,
                 o_ref, kp_ref, vp_ref):
    n = pl.program_id(1)
    blk = ATTN_BLOCK
    hd = A_HEAD_DIM

    @pl.when(n == 0)
    def _():
        kp_ref[...] = jnp.zeros_like(kp_ref)
        vp_ref[...] = jnp.zeros_like(vp_ref)

    c_tab = c_ref[...]
    s1_tab = s1_ref[...]
    s2_tab = s2_ref[...]
    lane = lax.broadcasted_iota(jnp.int32, (blk, LANES), 1)
    row = lax.broadcasted_iota(jnp.int32, (blk, LANES), 0)
    lo = lane < hd

    def norm_rope(xs, ws):
        sq = [x * x for x in xs]
        s_lo = [jnp.sum(jnp.where(lo, ss, 0.0), axis=-1, keepdims=True) for ss in sq]
        s_hi = [jnp.sum(jnp.where(lo, 0.0, ss), axis=-1, keepdims=True) for ss in sq]
        ys = [x * lax.rsqrt(jnp.where(lo, a, b) * (1.0 / hd) + EPS) * w
              for x, a, b, w in zip(xs, s_lo, s_hi, ws)]
        up = [pltpu.roll(y, LANES - ROT_DIM // 2, 1) for y in ys]
        down = [pltpu.roll(y, ROT_DIM // 2, 1) for y in ys]
        return [y * c_tab + u * s1_tab + d * s2_tab for y, u, d in zip(ys, up, down)]

    def both_halves(x):
        sw = pltpu.roll(x, hd, 1)
        return jnp.where(lo, x, sw), jnp.where(lo, sw, x)

    n_kpair = A_KV_WIDTH // LANES
    n_qpair = A_WIDTH // LANES
    roped = norm_rope(
        [kv_ref[:, kc * LANES:(kc + 1) * LANES].astype(F32) for kc in range(n_kpair)]
        + [q_ref[:, c * LANES:(c + 1) * LANES].astype(F32) for c in range(n_qpair)],
        [kw_ref[...]] * n_kpair + [qw_ref[...]] * n_qpair)
    k_cat, v_cat = [], []
    for kc in range(n_kpair):
        v_pair = kv_ref[:, A_KV_WIDTH + kc * LANES:A_KV_WIDTH + (kc + 1) * LANES].astype(F32)
        for hh, (k_dup, v_dup) in enumerate(zip(both_halves(roped[kc]), both_halves(v_pair))):
            g = 2 * kc + hh
            k_dup = k_dup.astype(BF16)
            v_dup = v_dup.astype(BF16)
            k_cat.append(jnp.concatenate([kp_ref[g], k_dup], axis=0))
            v_cat.append(jnp.concatenate([vp_ref[g], v_dup], axis=0))
            kp_ref[g] = k_dup
            vp_ref[g] = v_dup

    q_pairs = [(y * (hd ** -0.5)).astype(BF16) for y in roped[n_kpair:]]
    zero16 = jnp.zeros((blk, LANES), BF16)

    upper = lane > row
    prev_bias = jnp.where(n > 0, 0.0, NEG_BIG)

    groups = [range(g * A_GROUP, (g + 1) * A_GROUP) for g in range(A_KV_HEADS)]
    s_all = []
    for g, heads in enumerate(groups):
        q_stack = jnp.concatenate(
            [jnp.where(lo, q_pairs[h // 2], zero16) if h % 2 == 0
             else jnp.where(lo, zero16, q_pairs[h // 2]) for h in heads], axis=0)
        s_all.append(_dot_nt(q_stack, k_cat[g]))
    p_all, dens = [], [None] * A_Q_HEADS
    for g, heads in enumerate(groups):
        p_rows = []
        for r, h in enumerate(heads):
            s_r = s_all[g][r * blk:(r + 1) * blk]
            s = jnp.where(upper, s_r[:, :blk] + prev_bias, s_r[:, blk:])
            sink = sink_ref[h]
            m = jnp.maximum(jnp.max(s, axis=-1, keepdims=True), sink)
            p = jnp.exp(s - m)
            dens[h] = jnp.sum(p, axis=-1, keepdims=True) + jnp.exp(sink - m)
            p16 = p.astype(BF16)
            p_rows.append(jnp.concatenate([jnp.where(upper, p16, zero16),
                                           jnp.where(upper, zero16, p16)], axis=-1))
        p_all.append(jnp.concatenate(p_rows, axis=0))
    head_out = [None] * A_Q_HEADS
    for g, heads in enumerate(groups):
        o_all = _dot(p_all[g], v_cat[g])
        for r, h in enumerate(heads):
            head_out[h] = o_all[r * blk:(r + 1) * blk] / dens[h]

    for c in range(A_WIDTH // LANES):
        z = za_ref[:, c * LANES:(c + 1) * LANES].astype(F32)
        o_pair = jnp.where(lo, head_out[2 * c], head_out[2 * c + 1])
        o_ref[:, c * LANES:(c + 1) * LANES] = (o_pair * _silu(z)).astype(o_ref.dtype)


def _attention(p3, tabs, q_norm_w, k_norm_w, sinks):
    bsz, seq, _ = p3.shape
    blk = ATTN_BLOCK
    tile2 = lambda w: jnp.concatenate([w, w]).reshape(1, LANES).astype(F32)
    tab_spec = pl.BlockSpec((None, blk, LANES), lambda b, n: (b, n, 0))
    vec_spec = pl.BlockSpec((1, LANES), lambda b, n: (0, 0))
    return pl.pallas_call(
        _attn_kernel,
        grid=(bsz, seq // blk),
        in_specs=[
            pl.BlockSpec(memory_space=pltpu.SMEM),
            pl.BlockSpec((None, blk, A_WIDTH), lambda b, n: (b, n, COL_QA // A_WIDTH)),
            pl.BlockSpec((None, blk, A_WIDTH), lambda b, n: (b, n, COL_ZA // A_WIDTH)),
            pl.BlockSpec((None, blk, 2 * A_KV_WIDTH),
                         lambda b, n: (b, n, COL_KVA // (2 * A_KV_WIDTH))),
            tab_spec, tab_spec, tab_spec, vec_spec, vec_spec,
        ],
        out_specs=pl.BlockSpec((None, blk, A_WIDTH), lambda b, n: (b, n, 0)),
        out_shape=jax.ShapeDtypeStruct((bsz, seq, A_WIDTH), BF16),
        scratch_shapes=[pltpu.VMEM((A_KV_HEADS, blk, LANES), BF16),
                        pltpu.VMEM((A_KV_HEADS, blk, LANES), BF16)],
        compiler_params=pltpu.CompilerParams(
            dimension_semantics=("parallel", "arbitrary")),
        name="swa_attention",
    )(sinks.astype(F32), p3, p3, p3, *tabs, tile2(q_norm_w), tile2(k_norm_w))


B_HALO = 32
B_ROW_CHUNK = 64


def _convb_kernel(ub_ref, zb_ref, cw_ref, cb_ref, lnw_ref, lnb_ref, pw_ref, pb_ref,
                  o_ref, hbuf):
    s = pl.program_id(1)
    ts = ub_ref.shape[0]
    span = B_HALO + ts - SUBLANES

    @pl.when(s == 0)
    def _():
        hbuf[0, 0:B_HALO, :] = jnp.zeros((B_HALO, B_WIDTH), F32)

    @pl.when(s > 0)
    def _():
        hbuf[0, 0:B_HALO, :] = hbuf[0, ts:ts + B_HALO, :]

    u = ub_ref[...].astype(F32)
    hbuf[0, B_HALO:B_HALO + ts, :] = u[:, :B_WIDTH] * _sigmoid(u[:, B_WIDTH:])
    for r in range(1, SUBLANES):
        hbuf[r, 0:span, :] = hbuf[0, r:r + span, :]

    cb = cb_ref[...]
    lnw = lnw_ref[...]
    lnb = lnb_ref[...]
    pb = pb_ref[...]
    first = B_HALO - (B_CONV_WIDTH - 1)
    for r0 in range(0, ts, B_ROW_CHUNK):
        acc = jnp.broadcast_to(cb, (B_ROW_CHUNK, B_WIDTH))
        for k in range(B_CONV_WIDTH):
            shift = (first + k) % SUBLANES
            base = r0 + first + k - shift
            acc = acc + cw_ref[k:k + 1, :] * hbuf[shift, base:base + B_ROW_CHUNK, :]
        mu = jnp.mean(acc, axis=-1, keepdims=True)
        cen = acc - mu
        var = jnp.mean(cen * cen, axis=-1, keepdims=True)
        y = _silu(cen * lax.rsqrt(var + EPS) * lnw + lnb)
        out = _dot(y.astype(BF16), pw_ref[...]) + pb
        z = zb_ref[r0:r0 + B_ROW_CHUNK, :].astype(F32)
        o_ref[r0:r0 + B_ROW_CHUNK, :] = (out * _silu(z)).astype(o_ref.dtype)


def _conv_module(p3, conv_w, conv_b, ln_w, ln_b, pw_w, pw_b):
    bsz, seq, _ = p3.shape
    ts = min(256, seq)
    row = lambda v: v.reshape(1, B_WIDTH).astype(F32)
    vec_spec = pl.BlockSpec((1, B_WIDTH), lambda b, s: (0, 0))
    return pl.pallas_call(
        _convb_kernel,
        grid=(bsz, seq // ts),
        in_specs=[
            pl.BlockSpec((None, ts, 2 * B_WIDTH), lambda b, s: (b, s, COL_UB // (2 * B_WIDTH))),
            pl.BlockSpec((None, ts, B_WIDTH), lambda b, s: (b, s, COL_ZB // B_WIDTH)),
            pl.BlockSpec((B_CONV_WIDTH, B_WIDTH), lambda b, s: (0, 0)),
            vec_spec, vec_spec, vec_spec,
            pl.BlockSpec((B_WIDTH, B_WIDTH), lambda b, s: (0, 0)),
            vec_spec,
        ],
        out_specs=pl.BlockSpec((None, ts, B_WIDTH), lambda b, s: (b, s, 0)),
        out_shape=jax.ShapeDtypeStruct((bsz, seq, B_WIDTH), BF16),
        scratch_shapes=[pltpu.VMEM((SUBLANES, ts + B_HALO, B_WIDTH), F32)],
        compiler_params=pltpu.CompilerParams(
            dimension_semantics=("parallel", "arbitrary")),
        name="conformer_conv",
    )(p3, p3, conv_w.astype(F32), row(conv_b), row(ln_w), row(ln_b),
      pw_w.astype(BF16), row(pw_b))


C_SUB = 2 * CHUNK
C_PROJ_TILE = 256
C_HALO = 8
C_HEAD_GROUP = 6


def _gdn_mix(heads, xb, zb, r0, out_row, cw_ref, al_ref, dt_ref, ow_ref, o_ref, s_ref):
    ts = C_SUB
    dk = C_HEAD_DIM
    nchunk = ts // CHUNK

    def conv_silu(col0):
        cols = slice(col0, col0 + dk)
        first = C_HALO + r0 - (C_CONV_WIDTH - 1)
        acc = cw_ref[0:1, cols] * xb[first:first + ts, cols]
        for k in range(1, C_CONV_WIDTH):
            acc = acc + cw_ref[k:k + 1, cols] * xb[first + k:first + k + ts, cols]
        return _silu(acc)

    def l2n(x):
        return x * lax.rsqrt(jnp.sum(x * x, axis=-1, keepdims=True) + EPS)

    lane = lax.broadcasted_iota(jnp.int32, (ts, LANES), 1)
    row = lax.broadcasted_iota(jnp.int32, (ts, LANES), 0)
    ba = zb[r0:r0 + ts, C_WIDTH:C_WIDTH + LANES]
    beta = _sigmoid(ba)
    sp_in = ba + dt_ref[...]
    softplus = jnp.maximum(sp_in, 0.0) + jnp.log1p(jnp.exp(-jnp.abs(sp_in)))
    decay_lanes = (lane >= C_HEADS) & (lane < 2 * C_HEADS)
    g_log = jnp.where(decay_lanes, -jnp.exp(al_ref[...]) * softplus, 0.0)

    in_chunk = row & (CHUNK - 1)
    g_cum = g_log
    sh = 1
    while sh < CHUNK:
        g_cum = g_cum + jnp.where(in_chunk >= sh, pltpu.roll(g_cum, sh, 0), 0.0)
        sh *= 2
    g_cum_t = g_cum.T
    g_end = jnp.concatenate(
        [jnp.broadcast_to(g_cum[(c + 1) * CHUNK - 1:(c + 1) * CHUNK, :], (CHUNK, LANES))
         for c in range(nchunk)], axis=0)

    same_chunk = (row >> CHUNK_SHIFT) == (lane >> CHUNK_SHIFT)
    incl = same_chunk & (row >= lane)
    strict = same_chunk & (row > lane)
    eye = jnp.where(row == lane, 1.0, 0.0).astype(F32)
    ow = ow_ref[...]

    q, k, v, gcol, xp, inv, intra, rhs, kdec_t, lhs_s = ({} for _ in range(10))
    for h in heads:
        q[h] = l2n(conv_silu(h * dk)) * (dk ** -0.5)
        k[h] = l2n(conv_silu(C_WIDTH + h * dk))
        v[h] = conv_silu(2 * C_WIDTH + h * dk)
    for h in heads:
        gcol[h] = g_cum[:, C_HEADS + h:C_HEADS + h + 1]
        grow = g_cum_t[C_HEADS + h:C_HEADS + h + 1, :]
        bcol = beta[:, h:h + 1]
        decay = jnp.exp(jnp.where(incl, gcol[h] - grow, NEG_BIG))
        kb = k[h] * bcol
        egc = jnp.exp(gcol[h])
        k16 = k[h].astype(BF16)
        kk_qk = _dot_nt(jnp.concatenate([kb.astype(BF16), q[h].astype(BF16)], axis=0), k16)
        xp[h] = -jnp.where(strict, kk_qk[:ts] * decay, 0.0)
        intra[h] = (kk_qk[ts:] * decay).astype(BF16)
        rhs[h] = jnp.concatenate([(v[h] * bcol).astype(BF16), (kb * egc).astype(BF16)], axis=-1)
        kdec_t[h] = (k[h] * jnp.exp(g_end[:, C_HEADS + h:C_HEADS + h + 1] - gcol[h])).T.astype(BF16)
        lhs_s[h] = (q[h] * egc).astype(BF16)

    for h in heads:
        inv[h] = eye + xp[h]
        x16 = xp[h].astype(BF16)
        xp[h] = _dot(x16, x16)
    for level in range(1, 5):
        for h in heads:
            x16 = xp[h].astype(BF16)
            both = _dot(jnp.concatenate([x16, inv[h].astype(BF16)], axis=0), x16)
            xp[h] = both[:ts]
            inv[h] = inv[h] + both[ts:]
    u, w = {}, {}
    for h in heads:
        inv[h] = inv[h] + _dot(inv[h].astype(BF16), xp[h].astype(BF16))
    for h in heads:
        uw = _dot(inv[h].astype(BF16), rhs[h])
        u[h] = uw[:, :dk]
        w[h] = uw[:, dk:].astype(BF16)

    state = {h: s_ref[h] for h in heads}
    zeros_c = jnp.zeros((CHUNK, dk), BF16)
    for c in range(nchunk):
        rows = slice(c * CHUNK, (c + 1) * CHUNK)
        v_pad, o_inter = {}, {}
        for h in heads:
            ws_qs = _dot(jnp.concatenate([w[h][rows], lhs_s[h][rows]], axis=0),
                         state[h].astype(BF16))
            v_new = (u[h][rows] - ws_qs[:CHUNK]).astype(BF16)
            o_inter[h] = ws_qs[CHUNK:]
            v_pad[h] = jnp.concatenate([v_new if cc == c else zeros_c for cc in range(nchunk)], axis=0)
        for h in heads:
            av_kv = _dot(jnp.concatenate([intra[h][rows], kdec_t[h]], axis=0), v_pad[h])
            o = o_inter[h] + av_kv[:CHUNK]
            g_last = gcol[h][(c + 1) * CHUNK - 1:(c + 1) * CHUNK, :]
            state[h] = state[h] * jnp.exp(g_last) + av_kv[CHUNK:]
            on = o * lax.rsqrt(jnp.mean(o * o, axis=-1, keepdims=True) + EPS) * ow
            z = zb[r0 + c * CHUNK:r0 + (c + 1) * CHUNK, h * dk:(h + 1) * dk]
            o_ref[pl.ds(out_row + c * CHUNK, CHUNK), h * dk:(h + 1) * dk] = (
                (on * _silu(z)).astype(o_ref.dtype))
    for h in heads:
        s_ref[h] = state[h]


def _gdn_kernel(h_ref, w_ref, cw_ref, al_ref, dt_ref, ow_ref, o_ref, xb0, xb1, zb0, zb1, s_ref):
    pt = C_PROJ_TILE
    ntile = h_ref.shape[0] // pt
    assert ntile % 2 == 0

    def project(tile, xb, zb, xb_prev):
        rows = pl.ds(pl.multiple_of(tile * pt, pt), pt)
        hh = h_ref[rows, :]
        xb[C_HALO:C_HALO + pt, :] = _dot(hh, w_ref[:, :CCOL_Z])
        zb[...] = _dot(hh, w_ref[:, CCOL_Z:])
        if xb_prev is None:
            xb[0:C_HALO, :] = jnp.zeros((C_HALO, CCOL_Z), F32)
        else:
            xb[0:C_HALO, :] = xb_prev[pt:pt + C_HALO, :]

    def mix(tile, xb, zb):
        for r0 in range(0, pt, C_SUB):
            out_row = pl.multiple_of(tile * pt + r0, C_SUB)
            for h0 in range(0, C_HEADS, C_HEAD_GROUP):
                _gdn_mix(range(h0, h0 + C_HEAD_GROUP), xb, zb, r0, out_row,
                         cw_ref, al_ref, dt_ref, ow_ref, o_ref, s_ref)

    s_ref[...] = jnp.zeros_like(s_ref)
    project(0, xb0, zb0, None)

    def pair(j, carry):
        project(2 * j + 1, xb1, zb1, xb0)
        mix(2 * j, xb0, zb0)
        project(jnp.minimum(2 * j + 2, ntile - 1), xb0, zb0, xb1)
        mix(2 * j + 1, xb1, zb1)
        return carry

    lax.fori_loop(0, ntile // 2, pair, 0)


def _gated_deltanet(h3, w_c, conv_w, a_log, dt_bias, onorm_w):
    bsz, seq, d = h3.shape
    pt = C_PROJ_TILE
    pad = jnp.zeros((LANES - 2 * C_HEADS,), F32)
    zeros_h = jnp.zeros((C_HEADS,), F32)
    al_vec = jnp.concatenate([zeros_h, a_log.astype(F32), pad]).reshape(1, LANES)
    dt_vec = jnp.concatenate([zeros_h, dt_bias.astype(F32), pad]).reshape(1, LANES)
    vec_spec = pl.BlockSpec((1, LANES), lambda b: (0, 0))
    return pl.pallas_call(
        _gdn_kernel,
        grid=(bsz,),
        in_specs=[
            pl.BlockSpec((None, seq, d), lambda b: (b, 0, 0)),
            pl.BlockSpec((d, C_COLS), lambda b: (0, 0), pipeline_mode=pl.Buffered(1)),
            pl.BlockSpec((C_CONV_WIDTH, 3 * C_WIDTH), lambda b: (0, 0)),
            vec_spec, vec_spec, vec_spec,
        ],
        out_specs=pl.BlockSpec((None, seq, C_WIDTH), lambda b: (b, 0, 0)),
        out_shape=jax.ShapeDtypeStruct((bsz, seq, C_WIDTH), BF16),
        scratch_shapes=[pltpu.VMEM((C_HALO + pt, 3 * C_WIDTH), F32),
                        pltpu.VMEM((C_HALO + pt, 3 * C_WIDTH), F32),
                        pltpu.VMEM((pt, C_WIDTH + LANES), F32),
                        pltpu.VMEM((pt, C_WIDTH + LANES), F32),
                        pltpu.VMEM((C_HEADS, C_HEAD_DIM, C_HEAD_DIM), F32)],
        compiler_params=pltpu.CompilerParams(
            dimension_semantics=("parallel",), vmem_limit_bytes=VMEM_LIMIT),
        name="gated_deltanet",
    )(h3, w_c, conv_w.astype(F32), al_vec, dt_vec, onorm_w.reshape(1, LANES).astype(F32))


def _outproj_kernel(x_ref, ya_ref, yb_ref, yc_ref, wa_ref, wb_ref, wc_ref, o_ref):
    acc = _dot(ya_ref[...], wa_ref[...])
    acc = acc + _dot(yb_ref[...], wb_ref[...])
    acc = acc + _dot(yc_ref[...], wc_ref[...])
    o_ref[...] = x_ref[...] + acc


def _out_proj(x2d, ya, yb, yc, w_out):
    n, d = x2d.shape
    tm = min(512, n)
    w16 = w_out.astype(BF16)
    wa, wb, wc = w16[:A_WIDTH], w16[A_WIDTH:A_WIDTH + B_WIDTH], w16[A_WIDTH + B_WIDTH:]
    row_spec = lambda width: pl.BlockSpec((tm, width), lambda i: (i, 0))
    w_spec = lambda width: pl.BlockSpec((width, d), lambda i: (0, 0))
    return pl.pallas_call(
        _outproj_kernel,
        grid=(n // tm,),
        in_specs=[row_spec(d), row_spec(A_WIDTH), row_spec(B_WIDTH), row_spec(C_WIDTH),
                  w_spec(A_WIDTH), w_spec(B_WIDTH), w_spec(C_WIDTH)],
        out_specs=row_spec(d),
        out_shape=jax.ShapeDtypeStruct((n, d), F32),
        compiler_params=pltpu.CompilerParams(
            dimension_semantics=("parallel",),
            vmem_limit_bytes=VMEM_LIMIT),
        name="out_proj",
    )(x2d, ya, yb, yc, wa, wb, wc)


_IN_WIDTHS = (A_WIDTH, A_KV_WIDTH, A_KV_WIDTH, A_WIDTH, 2 * B_WIDTH, B_WIDTH,
              3 * C_WIDTH, C_HEADS, C_HEADS, C_WIDTH)
_IN_NAMES = ("qa", "ka", "va", "za", "ub", "zb", "qkv_c", "b_c", "a_c", "zc")
_IN_START = {}
_off = 0
for _name, _width in zip(_IN_NAMES, _IN_WIDTHS):
    _IN_START[_name] = _off
    _off += _width
IN_COLS = _off
_AB_LAYOUT = (
    (COL_UB, "ub", 2 * B_WIDTH), (COL_ZB, "zb", B_WIDTH), (COL_QA, "qa", A_WIDTH),
    (COL_ZA, "za", A_WIDTH), (COL_KVA, "ka", A_KV_WIDTH), (COL_KVA + A_KV_WIDTH, "va", A_KV_WIDTH))
_C_LAYOUT = ((0, "qkv_c", 3 * C_WIDTH), (CCOL_Z, "zc", C_WIDTH))


def _prep_kernel(w_ref, ab_ref, c_ref):
    for out_ref, layout in ((ab_ref, _AB_LAYOUT), (c_ref, _C_LAYOUT)):
        for dst, name, width in layout:
            src = _IN_START[name]
            if src % LANES == 0:
                out_ref[:, dst:dst + width] = w_ref[:, src:src + width].astype(BF16)
            else:
                base = src - src % LANES
                assert src + width == IN_COLS
                tail = w_ref[:, base:IN_COLS]
                out_ref[:, dst:dst + width] = tail[:, src - base:].astype(BF16)
    b0 = _IN_START["b_c"]
    assert b0 % LANES == 0 and b0 + LANES <= IN_COLS
    lane = lax.broadcasted_iota(jnp.int32, (w_ref.shape[0], LANES), 1)
    c_ref[:, CCOL_BA:CCOL_BA + LANES] = jnp.where(
        lane < 2 * C_HEADS, w_ref[:, b0:b0 + LANES], 0.0).astype(BF16)


def _prep_w_in(w_in, layer):
    _, d, cols = w_in.shape
    tr = 256
    return pl.pallas_call(
        _prep_kernel,
        grid=(d // tr,),
        in_specs=[pl.BlockSpec((None, tr, cols), lambda i: (layer, i, 0))],
        out_specs=[pl.BlockSpec((tr, AB_COLS), lambda i: (i, 0)),
                   pl.BlockSpec((tr, C_COLS), lambda i: (i, 0))],
        out_shape=[jax.ShapeDtypeStruct((d, AB_COLS), BF16),
                   jax.ShapeDtypeStruct((d, C_COLS), BF16)],
        compiler_params=pltpu.CompilerParams(
            dimension_semantics=("parallel",), vmem_limit_bytes=VMEM_LIMIT),
        name="prep_w_in",
    )(w_in)


def _layer_parts(l, x2d, bsz, seq, prm, tabs=None):
    d = x2d.shape[1]
    if tabs is None:
        tabs = _rope_tables(prm["positions"])
    w_ab, w_c = _prep_w_in(prm["w_in"], l)
    p2d, h2d = _in_proj(x2d, prm["norm_w"][l], w_ab)
    p3 = p2d.reshape(bsz, seq, AB_COLS)
    ya = _attention(p3, tabs, prm["q_norm_w"][l], prm["k_norm_w"][l], prm["sinks"][l])
    yb = _conv_module(p3, prm["b_conv_w"][l], prm["b_conv_b"][l], prm["b_ln_w"][l],
                      prm["b_ln_b"][l], prm["b_pw_w"][l], prm["b_pw_b"][l])
    yc = _gated_deltanet(h2d.reshape(bsz, seq, d), w_c, prm["c_conv_w"][l], prm["c_a_log"][l],
                         prm["c_dt_bias"][l], prm["c_onorm_w"][l])
    xo = _out_proj(x2d, ya.reshape(bsz * seq, A_WIDTH), yb.reshape(bsz * seq, B_WIDTH),
                   yc.reshape(bsz * seq, C_WIDTH), prm["w_out"][l])
    return {"ya": ya, "yb": yb, "yc": yc, "xo": xo}


def kernel(x, positions, norm_w, w_in, q_norm_w, k_norm_w, sinks, b_conv_w, b_conv_b,
           b_ln_w, b_ln_b, b_pw_w, b_pw_b, c_conv_w, c_a_log, c_dt_bias, c_onorm_w, w_out):
    bsz, seq, d = x.shape
    prm = dict(positions=positions, norm_w=norm_w, w_in=w_in, q_norm_w=q_norm_w,
               k_norm_w=k_norm_w, sinks=sinks, b_conv_w=b_conv_w, b_conv_b=b_conv_b,
               b_ln_w=b_ln_w, b_ln_b=b_ln_b, b_pw_w=b_pw_w, b_pw_b=b_pw_b, c_conv_w=c_conv_w,
               c_a_log=c_a_log, c_dt_bias=c_dt_bias, c_onorm_w=c_onorm_w, w_out=w_out)
    tabs = _rope_tables(positions)
    x2d = x.reshape(bsz * seq, d)
    for l in range(w_in.shape[0]):
        x2d = _layer_parts(l, x2d, bsz, seq, prm, tabs)["xo"]
    return x2d.reshape(bsz, seq, d)
```

```python
import jax
import jax.numpy as jnp
from jax import lax
from jax.experimental import pallas as pl
from jax.experimental.pallas import tpu as pltpu

F32 = jnp.float32
BF16 = jnp.bfloat16

EPS = 1e-6
ROPE_THETA = 500000.0
A_HEAD_DIM = 64
A_Q_HEADS = 12
A_KV_HEADS = 4
A_GROUP = 3
A_WIDTH = A_Q_HEADS * A_HEAD_DIM
A_KV_WIDTH = A_KV_HEADS * A_HEAD_DIM
ATTN_BLOCK = 128
ROT_DIM = 16
B_WIDTH = 512
B_CONV_WIDTH = 31
C_HEAD_DIM = 128
C_HEADS = 6
C_WIDTH = C_HEADS * C_HEAD_DIM
C_CONV_WIDTH = 4
CHUNK = 64
CHUNK_SHIFT = CHUNK.bit_length() - 1
LANES = 128
SUBLANES = 8

COL_UB = 0
COL_ZB = 2 * B_WIDTH
COL_QA = COL_ZB + B_WIDTH
COL_ZA = COL_QA + A_WIDTH
COL_KVA = COL_ZA + A_WIDTH
AB_COLS = COL_KVA + 2 * A_KV_WIDTH
CCOL_Z = 3 * C_WIDTH
CCOL_BA = CCOL_Z + C_WIDTH
C_COLS = CCOL_BA + LANES

NEG_BIG = -1e30
VMEM_LIMIT = 56 * 1024 * 1024


def _sigmoid(x):
    return 1.0 / (1.0 + jnp.exp(-x))


def _silu(x):
    return x * _sigmoid(x)


def _dot(a, b):
    return jnp.dot(a, b, preferred_element_type=F32)


def _dot_nt(a, b):
    return lax.dot_general(a, b, (((1,), (1,)), ((), ())), preferred_element_type=F32)


def _rope_kernel(pos_ref, invf_ref, cos_ref, sin_ref):
    p = pos_ref[...].astype(F32)
    for f in range(ROT_DIM // 2):
        ang = p * invf_ref[f:f + 1, :]
        cos_ref[f] = jnp.cos(ang)
        sin_ref[f] = jnp.sin(ang)


def _rope_tables(positions):
    bsz, seq = positions.shape
    rows = bsz * seq // LANES
    nf = ROT_DIM // 2
    inv_freq = ROPE_THETA ** (-jnp.arange(0, ROT_DIM, 2, dtype=F32) / ROT_DIM)
    invf = jnp.broadcast_to(inv_freq[:, None], (nf, LANES))
    cos, sin = pl.pallas_call(
        _rope_kernel,
        out_shape=[jax.ShapeDtypeStruct((nf, rows, LANES), F32)] * 2,
    )(positions.reshape(rows, LANES), invf)
    cos = cos.reshape(nf, bsz, seq).transpose(1, 2, 0)
    sin = sin.reshape(nf, bsz, seq).transpose(1, 2, 0)
    return jnp.concatenate([cos, sin, jnp.zeros((bsz, seq, LANES - ROT_DIM), F32)], axis=-1)


NORM_ROW_CHUNK = 128


def _inproj_kernel(x_ref, nw_ref, w_ref, p_ref, h_ref):
    j = pl.program_id(1)

    @pl.when(j == 0)
    def _():
        nw = nw_ref[...]
        chunk = min(NORM_ROW_CHUNK, x_ref.shape[0])

        def body(r, carry):
            rows = pl.ds(pl.multiple_of(r * chunk, chunk), chunk)
            x = x_ref[rows, :]
            ms = jnp.mean(x * x, axis=-1, keepdims=True)
            h_ref[rows, :] = (x * lax.rsqrt(ms + EPS) * nw).astype(h_ref.dtype)
            return carry

        lax.fori_loop(0, x_ref.shape[0] // chunk, body, 0)

    p_ref[...] = _dot(h_ref[...], w_ref[...]).astype(p_ref.dtype)


def _in_proj(x2d, norm_w, w):
    n, d = x2d.shape
    cols = w.shape[1]
    tm = min(1024, n)
    tn = 512
    return pl.pallas_call(
        _inproj_kernel,
        grid=(n // tm, cols // tn),
        in_specs=[pl.BlockSpec((tm, d), lambda i, j: (i, 0)),
                  pl.BlockSpec((1, d), lambda i, j: (0, 0)),
                  pl.BlockSpec((d, tn), lambda i, j: (0, j))],
        out_specs=[pl.BlockSpec((tm, tn), lambda i, j: (i, j)),
                   pl.BlockSpec((tm, d), lambda i, j: (i, 0))],
        out_shape=[jax.ShapeDtypeStruct((n, cols), BF16),
                   jax.ShapeDtypeStruct((n, d), BF16)],
        compiler_params=pltpu.CompilerParams(
            dimension_semantics=("parallel", "arbitrary"),
            vmem_limit_bytes=VMEM_LIMIT),
        name="in_proj",
    )(x2d, norm_w.reshape(1, d).astype(F32), w)


def _attn_kernel(sink_ref, q_ref, za_ref, kv_ref, tab_ref, qw_ref, kw_ref,
                 o_ref, kp_ref, vp_ref):
    n = pl.program_id(1)
    blk = ATTN_BLOCK
    hd = A_HEAD_DIM
    half = ROT_DIM // 2

    @pl.when(n == 0)
    def _():
        kp_ref[...] = jnp.zeros_like(kp_ref)
        vp_ref[...] = jnp.zeros_like(vp_ref)

    lane = lax.broadcasted_iota(jnp.int32, (blk, LANES), 1)
    row = lax.broadcasted_iota(jnp.int32, (blk, LANES), 0)
    lo = lane < hd

    tab = tab_ref[...]
    in_head = lane & (hd - 1)
    first = in_head < half
    second = (in_head >= half) & (in_head < ROT_DIM)
    cos_l = jnp.where(first, tab, pltpu.roll(tab, half, 1))
    cos_h = jnp.where(first, pltpu.roll(tab, hd, 1), pltpu.roll(tab, hd + half, 1))
    c_tab = jnp.where(in_head < ROT_DIM, jnp.where(lo, cos_l, cos_h), 1.0)
    s1_tab = jnp.where(first, -jnp.where(lo, pltpu.roll(tab, LANES - half, 1),
                                         pltpu.roll(tab, hd - half, 1)), 0.0)
    s2_tab = jnp.where(second, jnp.where(lo, tab, pltpu.roll(tab, hd, 1)), 0.0)

    def norm_rope(xs, ws):
        sq = [x * x for x in xs]
        s_lo = [jnp.sum(jnp.where(lo, ss, 0.0), axis=-1, keepdims=True) for ss in sq]
        s_hi = [jnp.sum(jnp.where(lo, 0.0, ss), axis=-1, keepdims=True) for ss in sq]
        ys = [x * lax.rsqrt(jnp.where(lo, a, b) * (1.0 / hd) + EPS) * w
              for x, a, b, w in zip(xs, s_lo, s_hi, ws)]
        up = [pltpu.roll(y, LANES - ROT_DIM // 2, 1) for y in ys]
        down = [pltpu.roll(y, ROT_DIM // 2, 1) for y in ys]
        return [y * c_tab + u * s1_tab + d * s2_tab for y, u, d in zip(ys, up, down)]

    def both_halves(x):
        sw = pltpu.roll(x, hd, 1)
        return jnp.where(lo, x, sw), jnp.where(lo, sw, x)

    n_kpair = A_KV_WIDTH // LANES
    n_qpair = A_WIDTH // LANES
    roped = norm_rope(
        [kv_ref[:, kc * LANES:(kc + 1) * LANES].astype(F32) for kc in range(n_kpair)]
        + [q_ref[:, c * LANES:(c + 1) * LANES].astype(F32) for c in range(n_qpair)],
        [kw_ref[...]] * n_kpair + [qw_ref[...]] * n_qpair)
    k_cat, v_cat = [], []
    for kc in range(n_kpair):
        v_pair = kv_ref[:, A_KV_WIDTH + kc * LANES:A_KV_WIDTH + (kc + 1) * LANES].astype(F32)
        for hh, (k_dup, v_dup) in enumerate(zip(both_halves(roped[kc]), both_halves(v_pair))):
            g = 2 * kc + hh
            k_dup = k_dup.astype(BF16)
            v_dup = v_dup.astype(BF16)
            k_cat.append(jnp.concatenate([kp_ref[g], k_dup], axis=0))
            v_cat.append(jnp.concatenate([vp_ref[g], v_dup], axis=0))
            kp_ref[g] = k_dup
            vp_ref[g] = v_dup

    q_pairs = [(y * (hd ** -0.5)).astype(BF16) for y in roped[n_kpair:]]
    zero16 = jnp.zeros((blk, LANES), BF16)

    upper = lane > row
    prev_bias = jnp.where(n > 0, 0.0, NEG_BIG)

    groups = [range(g * A_GROUP, (g + 1) * A_GROUP) for g in range(A_KV_HEADS)]
    s_all = []
    for g, heads in enumerate(groups):
        q_stack = jnp.concatenate(
            [jnp.where(lo, q_pairs[h // 2], zero16) if h % 2 == 0
             else jnp.where(lo, zero16, q_pairs[h // 2]) for h in heads], axis=0)
        s_all.append(_dot_nt(q_stack, k_cat[g]))
    p_all, dens = [], [None] * A_Q_HEADS
    for g, heads in enumerate(groups):
        p_rows = []
        for r, h in enumerate(heads):
            s_r = s_all[g][r * blk:(r + 1) * blk]
            s = jnp.where(upper, s_r[:, :blk] + prev_bias, s_r[:, blk:])
            sink = sink_ref[h]
            m = jnp.maximum(jnp.max(s, axis=-1, keepdims=True), sink)
            p = jnp.exp(s - m)
            dens[h] = jnp.sum(p, axis=-1, keepdims=True) + jnp.exp(sink - m)
            p16 = p.astype(BF16)
            p_rows.append(jnp.concatenate([jnp.where(upper, p16, zero16),
                                           jnp.where(upper, zero16, p16)], axis=-1))
        p_all.append(jnp.concatenate(p_rows, axis=0))
    head_out = [None] * A_Q_HEADS
    for g, heads in enumerate(groups):
        o_all = _dot(p_all[g], v_cat[g])
        for r, h in enumerate(heads):
            head_out[h] = o_all[r * blk:(r + 1) * blk] / dens[h]

    for c in range(A_WIDTH // LANES):
        z = za_ref[:, c * LANES:(c + 1) * LANES].astype(F32)
        o_pair = jnp.where(lo, head_out[2 * c], head_out[2 * c + 1])
        o_ref[:, c * LANES:(c + 1) * LANES] = (o_pair * _silu(z)).astype(o_ref.dtype)


def _attention(p3, tabs, q_norm_w, k_norm_w, sinks):
    bsz, seq, _ = p3.shape
    blk = ATTN_BLOCK
    tile2 = lambda w: jnp.concatenate([w, w]).reshape(1, LANES).astype(F32)
    tab_spec = pl.BlockSpec((None, blk, LANES), lambda b, n: (b, n, 0))
    vec_spec = pl.BlockSpec((1, LANES), lambda b, n: (0, 0))
    return pl.pallas_call(
        _attn_kernel,
        grid=(bsz, seq // blk),
        in_specs=[
            pl.BlockSpec(memory_space=pltpu.SMEM),
            pl.BlockSpec((None, blk, A_WIDTH), lambda b, n: (b, n, COL_QA // A_WIDTH)),
            pl.BlockSpec((None, blk, A_WIDTH), lambda b, n: (b, n, COL_ZA // A_WIDTH)),
            pl.BlockSpec((None, blk, 2 * A_KV_WIDTH),
                         lambda b, n: (b, n, COL_KVA // (2 * A_KV_WIDTH))),
            tab_spec, vec_spec, vec_spec,
        ],
        out_specs=pl.BlockSpec((None, blk, A_WIDTH), lambda b, n: (b, n, 0)),
        out_shape=jax.ShapeDtypeStruct((bsz, seq, A_WIDTH), BF16),
        scratch_shapes=[pltpu.VMEM((A_KV_HEADS, blk, LANES), BF16),
                        pltpu.VMEM((A_KV_HEADS, blk, LANES), BF16)],
        compiler_params=pltpu.CompilerParams(
            dimension_semantics=("parallel", "arbitrary")),
        name="swa_attention",
    )(sinks.astype(F32), p3, p3, p3, tabs, tile2(q_norm_w), tile2(k_norm_w))


B_HALO = 32
B_ROW_CHUNK = 64


def _convb_kernel(ub_ref, zb_ref, cw_ref, cb_ref, lnw_ref, lnb_ref, pw_ref, pb_ref,
                  o_ref, hbuf):
    s = pl.program_id(1)
    ts = ub_ref.shape[0]
    span = B_HALO + ts - SUBLANES

    @pl.when(s == 0)
    def _():
        hbuf[0, 0:B_HALO, :] = jnp.zeros((B_HALO, B_WIDTH), F32)

    @pl.when(s > 0)
    def _():
        hbuf[0, 0:B_HALO, :] = hbuf[0, ts:ts + B_HALO, :]

    u = ub_ref[...].astype(F32)
    hbuf[0, B_HALO:B_HALO + ts, :] = u[:, :B_WIDTH] * _sigmoid(u[:, B_WIDTH:])
    for r in range(1, SUBLANES):
        hbuf[r, 0:span, :] = hbuf[0, r:r + span, :]

    cb = cb_ref[...]
    lnw = lnw_ref[...]
    lnb = lnb_ref[...]
    pb = pb_ref[...]
    first = B_HALO - (B_CONV_WIDTH - 1)
    for r0 in range(0, ts, B_ROW_CHUNK):
        acc = jnp.broadcast_to(cb, (B_ROW_CHUNK, B_WIDTH))
        for k in range(B_CONV_WIDTH):
            shift = (first + k) % SUBLANES
            base = r0 + first + k - shift
            acc = acc + cw_ref[k:k + 1, :] * hbuf[shift, base:base + B_ROW_CHUNK, :]
        mu = jnp.mean(acc, axis=-1, keepdims=True)
        cen = acc - mu
        var = jnp.mean(cen * cen, axis=-1, keepdims=True)
        y = _silu(cen * lax.rsqrt(var + EPS) * lnw + lnb)
        out = _dot(y.astype(BF16), pw_ref[...]) + pb
        z = zb_ref[r0:r0 + B_ROW_CHUNK, :].astype(F32)
        o_ref[r0:r0 + B_ROW_CHUNK, :] = (out * _silu(z)).astype(o_ref.dtype)


def _conv_module(p3, conv_w, conv_b, ln_w, ln_b, pw_w, pw_b):
    bsz, seq, _ = p3.shape
    ts = min(256, seq)
    row = lambda v: v.reshape(1, B_WIDTH).astype(F32)
    vec_spec = pl.BlockSpec((1, B_WIDTH), lambda b, s: (0, 0))
    return pl.pallas_call(
        _convb_kernel,
        grid=(bsz, seq // ts),
        in_specs=[
            pl.BlockSpec((None, ts, 2 * B_WIDTH), lambda b, s: (b, s, COL_UB // (2 * B_WIDTH))),
            pl.BlockSpec((None, ts, B_WIDTH), lambda b, s: (b, s, COL_ZB // B_WIDTH)),
            pl.BlockSpec((B_CONV_WIDTH, B_WIDTH), lambda b, s: (0, 0)),
            vec_spec, vec_spec, vec_spec,
            pl.BlockSpec((B_WIDTH, B_WIDTH), lambda b, s: (0, 0)),
            vec_spec,
        ],
        out_specs=pl.BlockSpec((None, ts, B_WIDTH), lambda b, s: (b, s, 0)),
        out_shape=jax.ShapeDtypeStruct((bsz, seq, B_WIDTH), BF16),
        scratch_shapes=[pltpu.VMEM((SUBLANES, ts + B_HALO, B_WIDTH), F32)],
        compiler_params=pltpu.CompilerParams(
            dimension_semantics=("parallel", "arbitrary")),
        name="conformer_conv",
    )(p3, p3, conv_w.astype(F32), row(conv_b), row(ln_w), row(ln_b),
      pw_w.astype(BF16), row(pw_b))


C_SUB = 2 * CHUNK
C_PROJ_TILE = 256
C_PROJ_CHUNK = 256
C_CONV_FILL = 1
C_HALO = 8
C_HEAD_GROUP = 6


def _gdn_mix(heads, xb, zb, r0, out_row, cw_ref, al_ref, dt_ref, ow_ref, o_ref, s_ref, fill):
    ts = C_SUB
    dk = C_HEAD_DIM
    nchunk = ts // CHUNK

    def conv_silu(col0):
        cols = slice(col0, col0 + dk)
        first = C_HALO + r0 - (C_CONV_WIDTH - 1)
        acc = cw_ref[0:1, cols] * xb[first:first + ts, cols]
        for k in range(1, C_CONV_WIDTH):
            acc = acc + cw_ref[k:k + 1, cols] * xb[first + k:first + k + ts, cols]
        return _silu(acc)

    def l2n(x):
        return x * lax.rsqrt(jnp.sum(x * x, axis=-1, keepdims=True) + EPS)

    lane = lax.broadcasted_iota(jnp.int32, (ts, LANES), 1)
    row = lax.broadcasted_iota(jnp.int32, (ts, LANES), 0)
    ba = zb[r0:r0 + ts, C_WIDTH:C_WIDTH + LANES]
    beta = _sigmoid(ba)
    sp_in = ba + dt_ref[...]
    softplus = jnp.maximum(sp_in, 0.0) + jnp.log1p(jnp.exp(-jnp.abs(sp_in)))
    decay_lanes = (lane >= C_HEADS) & (lane < 2 * C_HEADS)
    g_log = jnp.where(decay_lanes, -jnp.exp(al_ref[...]) * softplus, 0.0)

    in_chunk = row & (CHUNK - 1)
    g_cum = g_log
    sh = 1
    while sh < CHUNK:
        g_cum = g_cum + jnp.where(in_chunk >= sh, pltpu.roll(g_cum, sh, 0), 0.0)
        sh *= 2
    g_cum_t = g_cum.T
    g_end = jnp.concatenate(
        [jnp.broadcast_to(g_cum[(c + 1) * CHUNK - 1:(c + 1) * CHUNK, :], (CHUNK, LANES))
         for c in range(nchunk)], axis=0)

    same_chunk = (row >> CHUNK_SHIFT) == (lane >> CHUNK_SHIFT)
    incl = same_chunk & (row >= lane)
    strict = same_chunk & (row > lane)
    eye = jnp.where(row == lane, 1.0, 0.0).astype(F32)
    ow = ow_ref[...]

    q, k, v, gcol, xp, inv, intra, rhs, kdec_t, lhs_s = ({} for _ in range(10))
    for h in heads:
        q[h] = l2n(conv_silu(h * dk)) * (dk ** -0.5)
        k[h] = l2n(conv_silu(C_WIDTH + h * dk))
        v[h] = conv_silu(2 * C_WIDTH + h * dk)
        fill(C_CONV_FILL)
    for h in heads:
        gcol[h] = g_cum[:, C_HEADS + h:C_HEADS + h + 1]
        grow = g_cum_t[C_HEADS + h:C_HEADS + h + 1, :]
        bcol = beta[:, h:h + 1]
        decay = jnp.exp(jnp.where(incl, gcol[h] - grow, NEG_BIG))
        kb = k[h] * bcol
        egc = jnp.exp(gcol[h])
        k16 = k[h].astype(BF16)
        kk_qk = _dot_nt(jnp.concatenate([kb.astype(BF16), q[h].astype(BF16)], axis=0), k16)
        xp[h] = -jnp.where(strict, kk_qk[:ts] * decay, 0.0)
        intra[h] = (kk_qk[ts:] * decay).astype(BF16)
        rhs[h] = jnp.concatenate([(v[h] * bcol).astype(BF16), (kb * egc).astype(BF16)], axis=-1)
        kdec_t[h] = (k[h] * jnp.exp(g_end[:, C_HEADS + h:C_HEADS + h + 1] - gcol[h])).T.astype(BF16)
        lhs_s[h] = (q[h] * egc).astype(BF16)
    fill()

    for h in heads:
        inv[h] = eye + xp[h]
        x16 = xp[h].astype(BF16)
        xp[h] = _dot(x16, x16)
    fill()
    for level in range(1, 5):
        for h in heads:
            x16 = xp[h].astype(BF16)
            both = _dot(jnp.concatenate([x16, inv[h].astype(BF16)], axis=0), x16)
            xp[h] = both[:ts]
            inv[h] = inv[h] + both[ts:]
        fill()
    u, w = {}, {}
    for h in heads:
        inv[h] = inv[h] + _dot(inv[h].astype(BF16), xp[h].astype(BF16))
    fill()
    for h in heads:
        uw = _dot(inv[h].astype(BF16), rhs[h])
        u[h] = uw[:, :dk]
        w[h] = uw[:, dk:].astype(BF16)
    fill()

    state = {h: s_ref[h] for h in heads}
    zeros_c = jnp.zeros((CHUNK, dk), BF16)
    for c in range(nchunk):
        rows = slice(c * CHUNK, (c + 1) * CHUNK)
        v_pad, o_inter = {}, {}
        for h in heads:
            ws_qs = _dot(jnp.concatenate([w[h][rows], lhs_s[h][rows]], axis=0),
                         state[h].astype(BF16))
            v_new = (u[h][rows] - ws_qs[:CHUNK]).astype(BF16)
            o_inter[h] = ws_qs[CHUNK:]
            v_pad[h] = jnp.concatenate([v_new if cc == c else zeros_c for cc in range(nchunk)], axis=0)
        fill()
        for h in heads:
            av_kv = _dot(jnp.concatenate([intra[h][rows], kdec_t[h]], axis=0), v_pad[h])
            o = o_inter[h] + av_kv[:CHUNK]
            g_last = gcol[h][(c + 1) * CHUNK - 1:(c + 1) * CHUNK, :]
            state[h] = state[h] * jnp.exp(g_last) + av_kv[CHUNK:]
            on = o * lax.rsqrt(jnp.mean(o * o, axis=-1, keepdims=True) + EPS) * ow
            z = zb[r0 + c * CHUNK:r0 + (c + 1) * CHUNK, h * dk:(h + 1) * dk]
            o_ref[pl.ds(out_row + c * CHUNK, CHUNK), h * dk:(h + 1) * dk] = (
                (on * _silu(z)).astype(o_ref.dtype))
        fill()
    for h in heads:
        s_ref[h] = state[h]


def _gdn_kernel(h_ref, w_ref, cw_ref, al_ref, dt_ref, ow_ref, o_ref, xb0, xb1, zb0, zb1, s_ref):
    pt = C_PROJ_TILE
    ntile = h_ref.shape[0] // pt
    assert ntile % 2 == 0

    def project_steps(tile, xb, zb, xb_prev):
        rows = pl.ds(pl.multiple_of(tile * pt, pt), pt)

        def halo():
            if xb_prev is None:
                xb[0:C_HALO, :] = jnp.zeros((C_HALO, CCOL_Z), F32)
            else:
                xb[0:C_HALO, :] = xb_prev[pt:pt + C_HALO, :]

        def chunk(c0, c1):
            def run():
                res = _dot(h_ref[rows, :], w_ref[:, c0:c1])
                if c0 < CCOL_Z:
                    xb[C_HALO:C_HALO + pt, c0:c1] = res
                else:
                    zb[:, c0 - CCOL_Z:c1 - CCOL_Z] = res
            return run

        bounds = list(range(0, CCOL_Z, C_PROJ_CHUNK)) + list(range(CCOL_Z, C_COLS, C_PROJ_CHUNK))
        ends = bounds[1:] + [C_COLS]
        return [halo] + [chunk(c0, c1) for c0, c1 in zip(bounds, ends)]

    def phase(steps, tile, xb, zb):
        units = [(r0, h0) for r0 in range(0, pt, C_SUB) for h0 in range(0, C_HEADS, C_HEAD_GROUP)]
        nslots = len(units) * (C_HEAD_GROUP * C_CONV_FILL + 8 + 2 * (C_SUB // CHUNK))
        pos = {"slot": 0, "step": 0}

        def fill(weight=1):
            pos["slot"] += weight
            target = min(len(steps), -(-pos["slot"] * len(steps) // nslots))
            while pos["step"] < target:
                steps[pos["step"]]()
                pos["step"] += 1

        for r0, h0 in units:
            out_row = pl.multiple_of(tile * pt + r0, C_SUB)
            _gdn_mix(range(h0, h0 + C_HEAD_GROUP), xb, zb, r0, out_row,
                     cw_ref, al_ref, dt_ref, ow_ref, o_ref, s_ref, fill)
        for step in steps[pos["step"]:]:
            step()

    s_ref[...] = jnp.zeros_like(s_ref)
    for step in project_steps(0, xb0, zb0, None):
        step()

    def pair(j, carry):
        phase(project_steps(2 * j + 1, xb1, zb1, xb0), 2 * j, xb0, zb0)
        phase(project_steps(jnp.minimum(2 * j + 2, ntile - 1), xb0, zb0, xb1), 2 * j + 1, xb1, zb1)
        return carry

    lax.fori_loop(0, ntile // 2, pair, 0)


def _gated_deltanet(h3, w_c, conv_w, a_log, dt_bias, onorm_w):
    bsz, seq, d = h3.shape
    pt = C_PROJ_TILE
    pad = jnp.zeros((LANES - 2 * C_HEADS,), F32)
    zeros_h = jnp.zeros((C_HEADS,), F32)
    al_vec = jnp.concatenate([zeros_h, a_log.astype(F32), pad]).reshape(1, LANES)
    dt_vec = jnp.concatenate([zeros_h, dt_bias.astype(F32), pad]).reshape(1, LANES)
    vec_spec = pl.BlockSpec((1, LANES), lambda b: (0, 0))
    return pl.pallas_call(
        _gdn_kernel,
        grid=(bsz,),
        in_specs=[
            pl.BlockSpec((None, seq, d), lambda b: (b, 0, 0)),
            pl.BlockSpec((d, C_COLS), lambda b: (0, 0), pipeline_mode=pl.Buffered(1)),
            pl.BlockSpec((C_CONV_WIDTH, 3 * C_WIDTH), lambda b: (0, 0)),
            vec_spec, vec_spec, vec_spec,
        ],
        out_specs=pl.BlockSpec((None, seq, C_WIDTH), lambda b: (b, 0, 0)),
        out_shape=jax.ShapeDtypeStruct((bsz, seq, C_WIDTH), BF16),
        scratch_shapes=[pltpu.VMEM((C_HALO + pt, 3 * C_WIDTH), F32),
                        pltpu.VMEM((C_HALO + pt, 3 * C_WIDTH), F32),
                        pltpu.VMEM((pt, C_WIDTH + LANES), F32),
                        pltpu.VMEM((pt, C_WIDTH + LANES), F32),
                        pltpu.VMEM((C_HEADS, C_HEAD_DIM, C_HEAD_DIM), F32)],
        compiler_params=pltpu.CompilerParams(
            dimension_semantics=("parallel",), vmem_limit_bytes=VMEM_LIMIT),
        name="gated_deltanet",
    )(h3, w_c, conv_w.astype(F32), al_vec, dt_vec, onorm_w.reshape(1, LANES).astype(F32))


def _outproj_kernel(x_ref, ya_ref, yb_ref, yc_ref, wa_ref, wb_ref, wc_ref, o_ref):
    acc = _dot(ya_ref[...], wa_ref[...])
    acc = acc + _dot(yb_ref[...], wb_ref[...])
    acc = acc + _dot(yc_ref[...], wc_ref[...])
    o_ref[...] = x_ref[...] + acc


def _out_proj(x2d, ya, yb, yc, w_out):
    n, d = x2d.shape
    tm = min(512, n)
    w16 = w_out.astype(BF16)
    wa, wb, wc = w16[:A_WIDTH], w16[A_WIDTH:A_WIDTH + B_WIDTH], w16[A_WIDTH + B_WIDTH:]
    row_spec = lambda width: pl.BlockSpec((tm, width), lambda i: (i, 0))
    w_spec = lambda width: pl.BlockSpec((width, d), lambda i: (0, 0))
    return pl.pallas_call(
        _outproj_kernel,
        grid=(n // tm,),
        in_specs=[row_spec(d), row_spec(A_WIDTH), row_spec(B_WIDTH), row_spec(C_WIDTH),
                  w_spec(A_WIDTH), w_spec(B_WIDTH), w_spec(C_WIDTH)],
        out_specs=row_spec(d),
        out_shape=jax.ShapeDtypeStruct((n, d), F32),
        compiler_params=pltpu.CompilerParams(
            dimension_semantics=("parallel",),
            vmem_limit_bytes=VMEM_LIMIT),
        name="out_proj",
    )(x2d, ya, yb, yc, wa, wb, wc)


_IN_WIDTHS = (A_WIDTH, A_KV_WIDTH, A_KV_WIDTH, A_WIDTH, 2 * B_WIDTH, B_WIDTH,
              3 * C_WIDTH, C_HEADS, C_HEADS, C_WIDTH)
_IN_NAMES = ("qa", "ka", "va", "za", "ub", "zb", "qkv_c", "b_c", "a_c", "zc")
_IN_START = {}
_off = 0
for _name, _width in zip(_IN_NAMES, _IN_WIDTHS):
    _IN_START[_name] = _off
    _off += _width
IN_COLS = _off
_AB_LAYOUT = (
    (COL_UB, "ub", 2 * B_WIDTH), (COL_ZB, "zb", B_WIDTH), (COL_QA, "qa", A_WIDTH),
    (COL_ZA, "za", A_WIDTH), (COL_KVA, "ka", A_KV_WIDTH), (COL_KVA + A_KV_WIDTH, "va", A_KV_WIDTH))
_C_LAYOUT = ((0, "qkv_c", 3 * C_WIDTH), (CCOL_Z, "zc", C_WIDTH))


def _prep_kernel(w_ref, ab_ref, c_ref):
    for out_ref, layout in ((ab_ref, _AB_LAYOUT), (c_ref, _C_LAYOUT)):
        for dst, name, width in layout:
            src = _IN_START[name]
            if src % LANES == 0:
                out_ref[:, dst:dst + width] = w_ref[:, src:src + width].astype(BF16)
            else:
                base = src - src % LANES
                assert src + width == IN_COLS
                tail = w_ref[:, base:IN_COLS]
                out_ref[:, dst:dst + width] = tail[:, src - base:].astype(BF16)
    b0 = _IN_START["b_c"]
    assert b0 % LANES == 0 and b0 + LANES <= IN_COLS
    lane = lax.broadcasted_iota(jnp.int32, (w_ref.shape[0], LANES), 1)
    c_ref[:, CCOL_BA:CCOL_BA + LANES] = jnp.where(
        lane < 2 * C_HEADS, w_ref[:, b0:b0 + LANES], 0.0).astype(BF16)


def _prep_w_in(w_in, layer):
    _, d, cols = w_in.shape
    tr = 256
    return pl.pallas_call(
        _prep_kernel,
        grid=(d // tr,),
        in_specs=[pl.BlockSpec((None, tr, cols), lambda i: (layer, i, 0))],
        out_specs=[pl.BlockSpec((tr, AB_COLS), lambda i: (i, 0)),
                   pl.BlockSpec((tr, C_COLS), lambda i: (i, 0))],
        out_shape=[jax.ShapeDtypeStruct((d, AB_COLS), BF16),
                   jax.ShapeDtypeStruct((d, C_COLS), BF16)],
        compiler_params=pltpu.CompilerParams(
            dimension_semantics=("parallel",), vmem_limit_bytes=VMEM_LIMIT),
        name="prep_w_in",
    )(w_in)


def _layer_parts(l, x2d, bsz, seq, prm, tabs=None):
    d = x2d.shape[1]
    if tabs is None:
        tabs = _rope_tables(prm["positions"])
    w_ab, w_c = _prep_w_in(prm["w_in"], l)
    p2d, h2d = _in_proj(x2d, prm["norm_w"][l], w_ab)
    p3 = p2d.reshape(bsz, seq, AB_COLS)
    ya = _attention(p3, tabs, prm["q_norm_w"][l], prm["k_norm_w"][l], prm["sinks"][l])
    yb = _conv_module(p3, prm["b_conv_w"][l], prm["b_conv_b"][l], prm["b_ln_w"][l],
                      prm["b_ln_b"][l], prm["b_pw_w"][l], prm["b_pw_b"][l])
    yc = _gated_deltanet(h2d.reshape(bsz, seq, d), w_c, prm["c_conv_w"][l], prm["c_a_log"][l],
                         prm["c_dt_bias"][l], prm["c_onorm_w"][l])
    xo = _out_proj(x2d, ya.reshape(bsz * seq, A_WIDTH), yb.reshape(bsz * seq, B_WIDTH),
                   yc.reshape(bsz * seq, C_WIDTH), prm["w_out"][l])
    return {"ya": ya, "yb": yb, "yc": yc, "xo": xo}


def kernel(x, positions, norm_w, w_in, q_norm_w, k_norm_w, sinks, b_conv_w, b_conv_b,
           b_ln_w, b_ln_b, b_pw_w, b_pw_b, c_conv_w, c_a_log, c_dt_bias, c_onorm_w, w_out):
    bsz, seq, d = x.shape
    prm = dict(positions=positions, norm_w=norm_w, w_in=w_in, q_norm_w=q_norm_w,
               k_norm_w=k_norm_w, sinks=sinks, b_conv_w=b_conv_w, b_conv_b=b_conv_b,
               b_ln_w=b_ln_w, b_ln_b=b_ln_b, b_pw_w=b_pw_w, b_pw_b=b_pw_b, c_conv_w=c_conv_w,
               c_a_log=c_a_log, c_dt_bias=c_dt_bias, c_onorm_w=c_onorm_w, w_out=w_out)
    tabs = _rope_tables(positions)
    x2d = x.reshape(bsz * seq, d)
    for l in range(w_in.shape[0]):
        x2d = _layer_parts(l, x2d, bsz, seq, prm, tabs)["xo"]
    return x2d.reshape(bsz, seq, d)
```

```python
import jax
import jax.numpy as jnp
from jax import lax
from jax.experimental import pallas as pl
from jax.experimental.pallas import tpu as pltpu

F32 = jnp.float32
BF16 = jnp.bfloat16

EPS = 1e-6
ROPE_THETA = 500000.0
A_HEAD_DIM = 64
A_Q_HEADS = 12
A_KV_HEADS = 4
A_GROUP = 3
A_WIDTH = A_Q_HEADS * A_HEAD_DIM
A_KV_WIDTH = A_KV_HEADS * A_HEAD_DIM
ATTN_BLOCK = 128
ROT_DIM = 16
B_WIDTH = 512
B_CONV_WIDTH = 31
C_HEAD_DIM = 128
C_HEADS = 6
C_WIDTH = C_HEADS * C_HEAD_DIM
C_CONV_WIDTH = 4
CHUNK = 64
CHUNK_SHIFT = CHUNK.bit_length() - 1
LANES = 128
SUBLANES = 8

COL_QA = 0
COL_ZA = COL_QA + A_WIDTH
COL_KVA = COL_ZA + A_WIDTH
A_COLS = COL_KVA + 2 * A_KV_WIDTH
BCOL_GATE = B_WIDTH
BCOL_Z = 2 * B_WIDTH
B_COLS = BCOL_Z + B_WIDTH
CCOL_Z = 3 * C_WIDTH
CCOL_BA = CCOL_Z + C_WIDTH
C_COLS = CCOL_BA + LANES

NEG_BIG = -1e30
VMEM_LIMIT = 56 * 1024 * 1024


def _sigmoid(x):
    return 1.0 / (1.0 + jnp.exp(-x))


def _silu(x):
    return x * _sigmoid(x)


def _dot(a, b):
    return jnp.dot(a, b, preferred_element_type=F32)


def _dot_nt(a, b):
    return lax.dot_general(a, b, (((1,), (1,)), ((), ())), preferred_element_type=F32)


def _rope_kernel(pos_ref, invf_ref, cos_ref, sin_ref):
    p = pos_ref[...].astype(F32)
    for f in range(ROT_DIM // 2):
        ang = p * invf_ref[f:f + 1, :]
        cos_ref[f] = jnp.cos(ang)
        sin_ref[f] = jnp.sin(ang)


def _rope_tables(positions):
    bsz, seq = positions.shape
    rows = bsz * seq // LANES
    nf = ROT_DIM // 2
    inv_freq = ROPE_THETA ** (-jnp.arange(0, ROT_DIM, 2, dtype=F32) / ROT_DIM)
    invf = jnp.broadcast_to(inv_freq[:, None], (nf, LANES))
    cos, sin = pl.pallas_call(
        _rope_kernel,
        out_shape=[jax.ShapeDtypeStruct((nf, rows, LANES), F32)] * 2,
    )(positions.reshape(rows, LANES), invf)
    cos = cos.reshape(nf, bsz, seq).transpose(1, 2, 0)
    sin = sin.reshape(nf, bsz, seq).transpose(1, 2, 0)
    return jnp.concatenate([cos, sin, jnp.zeros((bsz, seq, LANES - ROT_DIM), F32)], axis=-1)


NORM_ROW_CHUNK = 128


def _inproj_kernel(x_ref, nw_ref, w_ref, p_ref, h_ref):
    j = pl.program_id(1)

    @pl.when(j == 0)
    def _():
        nw = nw_ref[...]
        chunk = min(NORM_ROW_CHUNK, x_ref.shape[0])

        def body(r, carry):
            rows = pl.ds(pl.multiple_of(r * chunk, chunk), chunk)
            x = x_ref[rows, :]
            ms = jnp.mean(x * x, axis=-1, keepdims=True)
            h_ref[rows, :] = (x * lax.rsqrt(ms + EPS) * nw).astype(h_ref.dtype)
            return carry

        lax.fori_loop(0, x_ref.shape[0] // chunk, body, 0)

    p_ref[...] = _dot(h_ref[...], w_ref[...]).astype(p_ref.dtype)


def _in_proj(x2d, norm_w, w):
    n, d = x2d.shape
    cols = w.shape[1]
    tm = min(1024, n)
    tn = 512
    return pl.pallas_call(
        _inproj_kernel,
        grid=(n // tm, cols // tn),
        in_specs=[pl.BlockSpec((tm, d), lambda i, j: (i, 0)),
                  pl.BlockSpec((1, d), lambda i, j: (0, 0)),
                  pl.BlockSpec((d, tn), lambda i, j: (0, j))],
        out_specs=[pl.BlockSpec((tm, tn), lambda i, j: (i, j)),
                   pl.BlockSpec((tm, d), lambda i, j: (i, 0))],
        out_shape=[jax.ShapeDtypeStruct((n, cols), BF16),
                   jax.ShapeDtypeStruct((n, d), BF16)],
        compiler_params=pltpu.CompilerParams(
            dimension_semantics=("parallel", "arbitrary"),
            vmem_limit_bytes=VMEM_LIMIT),
        name="in_proj",
    )(x2d, norm_w.reshape(1, d).astype(F32), w)


def _attn_kernel(sink_ref, q_ref, za_ref, kv_ref, tab_ref, qw_ref, kw_ref,
                 o_ref, kp_ref, vp_ref):
    n = pl.program_id(1)
    blk = ATTN_BLOCK
    hd = A_HEAD_DIM
    half = ROT_DIM // 2

    @pl.when(n == 0)
    def _():
        kp_ref[...] = jnp.zeros_like(kp_ref)
        vp_ref[...] = jnp.zeros_like(vp_ref)

    lane = lax.broadcasted_iota(jnp.int32, (blk, LANES), 1)
    row = lax.broadcasted_iota(jnp.int32, (blk, LANES), 0)
    lo = lane < hd

    tab = tab_ref[...]
    in_head = lane & (hd - 1)
    first = in_head < half
    second = (in_head >= half) & (in_head < ROT_DIM)
    cos_l = jnp.where(first, tab, pltpu.roll(tab, half, 1))
    cos_h = jnp.where(first, pltpu.roll(tab, hd, 1), pltpu.roll(tab, hd + half, 1))
    c_tab = jnp.where(in_head < ROT_DIM, jnp.where(lo, cos_l, cos_h), 1.0)
    s1_tab = jnp.where(first, -jnp.where(lo, pltpu.roll(tab, LANES - half, 1),
                                         pltpu.roll(tab, hd - half, 1)), 0.0)
    s2_tab = jnp.where(second, jnp.where(lo, tab, pltpu.roll(tab, hd, 1)), 0.0)

    def norm_rope(xs, ws):
        sq = [x * x for x in xs]
        s_lo = [jnp.sum(jnp.where(lo, ss, 0.0), axis=-1, keepdims=True) for ss in sq]
        s_hi = [jnp.sum(jnp.where(lo, 0.0, ss), axis=-1, keepdims=True) for ss in sq]
        ys = [x * lax.rsqrt(jnp.where(lo, a, b) * (1.0 / hd) + EPS) * w
              for x, a, b, w in zip(xs, s_lo, s_hi, ws)]
        up = [pltpu.roll(y, LANES - ROT_DIM // 2, 1) for y in ys]
        down = [pltpu.roll(y, ROT_DIM // 2, 1) for y in ys]
        return [y * c_tab + u * s1_tab + d * s2_tab for y, u, d in zip(ys, up, down)]

    def both_halves(x):
        sw = pltpu.roll(x, hd, 1)
        return jnp.where(lo, x, sw), jnp.where(lo, sw, x)

    n_kpair = A_KV_WIDTH // LANES
    n_qpair = A_WIDTH // LANES
    roped = norm_rope(
        [kv_ref[:, kc * LANES:(kc + 1) * LANES].astype(F32) for kc in range(n_kpair)]
        + [q_ref[:, c * LANES:(c + 1) * LANES].astype(F32) for c in range(n_qpair)],
        [kw_ref[...]] * n_kpair + [qw_ref[...]] * n_qpair)
    k_cat, v_cat = [], []
    for kc in range(n_kpair):
        v_pair = kv_ref[:, A_KV_WIDTH + kc * LANES:A_KV_WIDTH + (kc + 1) * LANES].astype(F32)
        for hh, (k_dup, v_dup) in enumerate(zip(both_halves(roped[kc]), both_halves(v_pair))):
            g = 2 * kc + hh
            k_dup = k_dup.astype(BF16)
            v_dup = v_dup.astype(BF16)
            k_cat.append(jnp.concatenate([kp_ref[g], k_dup], axis=0))
            v_cat.append(jnp.concatenate([vp_ref[g], v_dup], axis=0))
            kp_ref[g] = k_dup
            vp_ref[g] = v_dup

    q_pairs = [(y * (hd ** -0.5)).astype(BF16) for y in roped[n_kpair:]]
    zero16 = jnp.zeros((blk, LANES), BF16)

    upper = lane > row
    prev_bias = jnp.where(n > 0, 0.0, NEG_BIG)

    groups = [range(g * A_GROUP, (g + 1) * A_GROUP) for g in range(A_KV_HEADS)]
    s_all = []
    for g, heads in enumerate(groups):
        q_stack = jnp.concatenate(
            [jnp.where(lo, q_pairs[h // 2], zero16) if h % 2 == 0
             else jnp.where(lo, zero16, q_pairs[h // 2]) for h in heads], axis=0)
        s_all.append(_dot_nt(q_stack, k_cat[g]))
    p_all, dens = [], [None] * A_Q_HEADS
    for g, heads in enumerate(groups):
        p_rows = []
        for r, h in enumerate(heads):
            s_r = s_all[g][r * blk:(r + 1) * blk]
            s = jnp.where(upper, s_r[:, :blk] + prev_bias, s_r[:, blk:])
            sink = sink_ref[h]
            m = jnp.maximum(jnp.max(s, axis=-1, keepdims=True), sink)
            p = jnp.exp(s - m)
            dens[h] = jnp.sum(p, axis=-1, keepdims=True) + jnp.exp(sink - m)
            p16 = p.astype(BF16)
            p_rows.append(jnp.concatenate([jnp.where(upper, p16, zero16),
                                           jnp.where(upper, zero16, p16)], axis=-1))
        p_all.append(jnp.concatenate(p_rows, axis=0))
    head_out = [None] * A_Q_HEADS
    for g, heads in enumerate(groups):
        o_all = _dot(p_all[g], v_cat[g])
        for r, h in enumerate(heads):
            head_out[h] = o_all[r * blk:(r + 1) * blk] / dens[h]

    for c in range(A_WIDTH // LANES):
        z = za_ref[:, c * LANES:(c + 1) * LANES].astype(F32)
        o_pair = jnp.where(lo, head_out[2 * c], head_out[2 * c + 1])
        o_ref[:, c * LANES:(c + 1) * LANES] = (o_pair * _silu(z)).astype(o_ref.dtype)


def _attention(p3, tabs, q_norm_w, k_norm_w, sinks):
    bsz, seq, _ = p3.shape
    blk = ATTN_BLOCK
    tile2 = lambda w: jnp.concatenate([w, w]).reshape(1, LANES).astype(F32)
    tab_spec = pl.BlockSpec((None, blk, LANES), lambda b, n: (b, n, 0))
    vec_spec = pl.BlockSpec((1, LANES), lambda b, n: (0, 0))
    return pl.pallas_call(
        _attn_kernel,
        grid=(bsz, seq // blk),
        in_specs=[
            pl.BlockSpec(memory_space=pltpu.SMEM),
            pl.BlockSpec((None, blk, A_WIDTH), lambda b, n: (b, n, COL_QA // A_WIDTH)),
            pl.BlockSpec((None, blk, A_WIDTH), lambda b, n: (b, n, COL_ZA // A_WIDTH)),
            pl.BlockSpec((None, blk, 2 * A_KV_WIDTH),
                         lambda b, n: (b, n, COL_KVA // (2 * A_KV_WIDTH))),
            tab_spec, vec_spec, vec_spec,
        ],
        out_specs=pl.BlockSpec((None, blk, A_WIDTH), lambda b, n: (b, n, 0)),
        out_shape=jax.ShapeDtypeStruct((bsz, seq, A_WIDTH), BF16),
        scratch_shapes=[pltpu.VMEM((A_KV_HEADS, blk, LANES), BF16),
                        pltpu.VMEM((A_KV_HEADS, blk, LANES), BF16)],
        compiler_params=pltpu.CompilerParams(
            dimension_semantics=("parallel", "arbitrary")),
        name="swa_attention",
    )(sinks.astype(F32), p3, p3, p3, tabs, tile2(q_norm_w), tile2(k_norm_w))


B_HALO = 32
B_ROW_CHUNK = 64
B_PROJ_TILE = 256
B_PROJ_CHUNK = 256
B_TAP_GROUP = 16


def _convb_mix(hb, zb, out_row0, cw_ref, cb_ref, lnw_ref, lnb_ref, pw_ref, pb_ref, o_ref, fill):
    cb = cb_ref[...]
    lnw = lnw_ref[...]
    lnb = lnb_ref[...]
    pb = pb_ref[...]
    first = B_HALO - (B_CONV_WIDTH - 1)
    chunks = range(0, zb.shape[0], B_ROW_CHUNK)
    accs = []
    for r0 in chunks:
        acc = jnp.broadcast_to(cb, (B_ROW_CHUNK, B_WIDTH))
        for k in range(B_CONV_WIDTH):
            shift = (first + k) % SUBLANES
            base = r0 + first + k - shift
            acc = acc + cw_ref[k:k + 1, :] * hb[shift, base:base + B_ROW_CHUNK, :]
            if k % B_TAP_GROUP == B_TAP_GROUP - 1:
                fill()
        accs.append(acc)
        fill()
    mus = [jnp.mean(acc, axis=-1, keepdims=True) for acc in accs]
    cens = [acc - mu for acc, mu in zip(accs, mus)]
    varis = [jnp.mean(cen * cen, axis=-1, keepdims=True) for cen in cens]
    ys = [_silu(cen * lax.rsqrt(var + EPS) * lnw + lnb).astype(BF16)
          for cen, var in zip(cens, varis)]
    outs = [_dot(y, pw_ref[...]) + pb for y in ys]
    for r0, out in zip(chunks, outs):
        z = zb[r0:r0 + B_ROW_CHUNK, :]
        o_ref[pl.ds(out_row0 + r0, B_ROW_CHUNK), :] = (out * _silu(z)).astype(o_ref.dtype)


def _convb_kernel(h_ref, w_ref, cw_ref, cb_ref, lnw_ref, lnb_ref, pw_ref, pb_ref, o_ref,
                  hb0, hb1, zb0, zb1):
    pt = B_PROJ_TILE
    ntile = h_ref.shape[0] // pt
    assert ntile % 2 == 0
    span = B_HALO + pt - SUBLANES

    def project_steps(tile, hb, zb, hb_prev):
        rows = pl.ds(pl.multiple_of(tile * pt, pt), pt)

        def halo():
            if hb_prev is None:
                hb[0, 0:B_HALO, :] = jnp.zeros((B_HALO, B_WIDTH), F32)
            else:
                hb[0, 0:B_HALO, :] = hb_prev[0, pt:pt + B_HALO, :]

        def glu(c0):
            def run():
                hh = h_ref[rows, :]
                a = _dot(hh, w_ref[:, c0:c0 + B_PROJ_CHUNK])
                g = _dot(hh, w_ref[:, BCOL_GATE + c0:BCOL_GATE + c0 + B_PROJ_CHUNK])
                hb[0, B_HALO:B_HALO + pt, c0:c0 + B_PROJ_CHUNK] = a * _sigmoid(g)
            return run

        def gate(c0):
            def run():
                zb[:, c0:c0 + B_PROJ_CHUNK] = _dot(
                    h_ref[rows, :], w_ref[:, BCOL_Z + c0:BCOL_Z + c0 + B_PROJ_CHUNK])
            return run

        def shifted(r):
            def run():
                hb[r, 0:span, :] = hb[0, r:r + span, :]
            return run

        cols = range(0, B_WIDTH, B_PROJ_CHUNK)
        return ([halo] + [glu(c0) for c0 in cols] + [gate(c0) for c0 in cols]
                + [shifted(r) for r in range(1, SUBLANES)])

    def phase(steps, tile, hb, zb):
        nslots = (pt // B_ROW_CHUNK) * (B_CONV_WIDTH // B_TAP_GROUP + 1)
        pos = {"slot": 0, "step": 0}

        def fill():
            pos["slot"] += 1
            target = min(len(steps), -(-pos["slot"] * len(steps) // nslots))
            while pos["step"] < target:
                steps[pos["step"]]()
                pos["step"] += 1

        _convb_mix(hb, zb, pl.multiple_of(tile * pt, pt), cw_ref, cb_ref, lnw_ref, lnb_ref,
                   pw_ref, pb_ref, o_ref, fill)
        for step in steps[pos["step"]:]:
            step()

    for step in project_steps(0, hb0, zb0, None):
        step()

    def pair(j, carry):
        phase(project_steps(2 * j + 1, hb1, zb1, hb0), 2 * j, hb0, zb0)
        phase(project_steps(jnp.minimum(2 * j + 2, ntile - 1), hb0, zb0, hb1), 2 * j + 1, hb1, zb1)
        return carry

    lax.fori_loop(0, ntile // 2, pair, 0)


def _conv_module(h3, w_b, conv_w, conv_b, ln_w, ln_b, pw_w, pw_b):
    bsz, seq, d = h3.shape
    pt = B_PROJ_TILE
    row = lambda v: v.reshape(1, B_WIDTH).astype(F32)
    vec_spec = pl.BlockSpec((1, B_WIDTH), lambda b: (0, 0))
    return pl.pallas_call(
        _convb_kernel,
        grid=(bsz,),
        in_specs=[
            pl.BlockSpec((None, seq, d), lambda b: (b, 0, 0)),
            pl.BlockSpec((d, B_COLS), lambda b: (0, 0)),
            pl.BlockSpec((B_CONV_WIDTH, B_WIDTH), lambda b: (0, 0)),
            vec_spec, vec_spec, vec_spec,
            pl.BlockSpec((B_WIDTH, B_WIDTH), lambda b: (0, 0)),
            vec_spec,
        ],
        out_specs=pl.BlockSpec((None, seq, B_WIDTH), lambda b: (b, 0, 0)),
        out_shape=jax.ShapeDtypeStruct((bsz, seq, B_WIDTH), BF16),
        scratch_shapes=[pltpu.VMEM((SUBLANES, pt + B_HALO, B_WIDTH), F32),
                        pltpu.VMEM((SUBLANES, pt + B_HALO, B_WIDTH), F32),
                        pltpu.VMEM((pt, B_WIDTH), F32),
                        pltpu.VMEM((pt, B_WIDTH), F32)],
        compiler_params=pltpu.CompilerParams(
            dimension_semantics=("parallel",), vmem_limit_bytes=VMEM_LIMIT),
        name="conformer_conv",
    )(h3, w_b, conv_w.astype(F32), row(conv_b), row(ln_w), row(ln_b),
      pw_w.astype(BF16), row(pw_b))


C_SUB = 2 * CHUNK
C_PROJ_TILE = 256
C_PROJ_CHUNK = 256
C_CONV_FILL = 1
C_HALO = 8
C_HEAD_GROUP = 6


def _gdn_mix(heads, xb, zb, r0, out_row, cw_ref, al_ref, dt_ref, ow_ref, o_ref, s_ref, fill):
    ts = C_SUB
    dk = C_HEAD_DIM
    nchunk = ts // CHUNK

    def conv_silu(col0):
        cols = slice(col0, col0 + dk)
        first = C_HALO + r0 - (C_CONV_WIDTH - 1)
        acc = cw_ref[0:1, cols] * xb[first:first + ts, cols]
        for k in range(1, C_CONV_WIDTH):
            acc = acc + cw_ref[k:k + 1, cols] * xb[first + k:first + k + ts, cols]
        return _silu(acc)

    def l2n(x):
        return x * lax.rsqrt(jnp.sum(x * x, axis=-1, keepdims=True) + EPS)

    lane = lax.broadcasted_iota(jnp.int32, (ts, LANES), 1)
    row = lax.broadcasted_iota(jnp.int32, (ts, LANES), 0)
    ba = zb[r0:r0 + ts, C_WIDTH:C_WIDTH + LANES]
    beta = _sigmoid(ba)
    sp_in = ba + dt_ref[...]
    softplus = jnp.maximum(sp_in, 0.0) + jnp.log1p(jnp.exp(-jnp.abs(sp_in)))
    decay_lanes = (lane >= C_HEADS) & (lane < 2 * C_HEADS)
    g_log = jnp.where(decay_lanes, -jnp.exp(al_ref[...]) * softplus, 0.0)

    in_chunk = row & (CHUNK - 1)
    g_cum = g_log
    sh = 1
    while sh < CHUNK:
        g_cum = g_cum + jnp.where(in_chunk >= sh, pltpu.roll(g_cum, sh, 0), 0.0)
        sh *= 2
    g_cum_t = g_cum.T
    g_end = jnp.concatenate(
        [jnp.broadcast_to(g_cum[(c + 1) * CHUNK - 1:(c + 1) * CHUNK, :], (CHUNK, LANES))
         for c in range(nchunk)], axis=0)

    same_chunk = (row >> CHUNK_SHIFT) == (lane >> CHUNK_SHIFT)
    incl = same_chunk & (row >= lane)
    strict = same_chunk & (row > lane)
    eye = jnp.where(row == lane, 1.0, 0.0).astype(F32)
    ow = ow_ref[...]

    q, k, v, gcol, xp, inv, intra, rhs, kdec_t, lhs_s = ({} for _ in range(10))
    for h in heads:
        q[h] = l2n(conv_silu(h * dk)) * (dk ** -0.5)
        k[h] = l2n(conv_silu(C_WIDTH + h * dk))
        v[h] = conv_silu(2 * C_WIDTH + h * dk)
        fill(C_CONV_FILL)
    for h in heads:
        gcol[h] = g_cum[:, C_HEADS + h:C_HEADS + h + 1]
        grow = g_cum_t[C_HEADS + h:C_HEADS + h + 1, :]
        bcol = beta[:, h:h + 1]
        decay = jnp.exp(jnp.where(incl, gcol[h] - grow, NEG_BIG))
        kb = k[h] * bcol
        egc = jnp.exp(gcol[h])
        k16 = k[h].astype(BF16)
        kk_qk = _dot_nt(jnp.concatenate([kb.astype(BF16), q[h].astype(BF16)], axis=0), k16)
        xp[h] = -jnp.where(strict, kk_qk[:ts] * decay, 0.0)
        intra[h] = (kk_qk[ts:] * decay).astype(BF16)
        rhs[h] = jnp.concatenate([(v[h] * bcol).astype(BF16), (kb * egc).astype(BF16)], axis=-1)
        kdec_t[h] = (k[h] * jnp.exp(g_end[:, C_HEADS + h:C_HEADS + h + 1] - gcol[h])).T.astype(BF16)
        lhs_s[h] = (q[h] * egc).astype(BF16)
    fill()

    for h in heads:
        inv[h] = eye + xp[h]
        x16 = xp[h].astype(BF16)
        xp[h] = _dot(x16, x16)
    fill()
    for level in range(1, 5):
        for h in heads:
            x16 = xp[h].astype(BF16)
            both = _dot(jnp.concatenate([x16, inv[h].astype(BF16)], axis=0), x16)
            xp[h] = both[:ts]
            inv[h] = inv[h] + both[ts:]
        fill()
    u, w = {}, {}
    for h in heads:
        inv[h] = inv[h] + _dot(inv[h].astype(BF16), xp[h].astype(BF16))
    fill()
    for h in heads:
        uw = _dot(inv[h].astype(BF16), rhs[h])
        u[h] = uw[:, :dk]
        w[h] = uw[:, dk:].astype(BF16)
    fill()

    state = {h: s_ref[h] for h in heads}
    zeros_c = jnp.zeros((CHUNK, dk), BF16)
    for c in range(nchunk):
        rows = slice(c * CHUNK, (c + 1) * CHUNK)
        v_pad, o_inter = {}, {}
        for h in heads:
            ws_qs = _dot(jnp.concatenate([w[h][rows], lhs_s[h][rows]], axis=0),
                         state[h].astype(BF16))
            v_new = (u[h][rows] - ws_qs[:CHUNK]).astype(BF16)
            o_inter[h] = ws_qs[CHUNK:]
            v_pad[h] = jnp.concatenate([v_new if cc == c else zeros_c for cc in range(nchunk)], axis=0)
        fill()
        for h in heads:
            av_kv = _dot(jnp.concatenate([intra[h][rows], kdec_t[h]], axis=0), v_pad[h])
            o = o_inter[h] + av_kv[:CHUNK]
            g_last = gcol[h][(c + 1) * CHUNK - 1:(c + 1) * CHUNK, :]
            state[h] = state[h] * jnp.exp(g_last) + av_kv[CHUNK:]
            on = o * lax.rsqrt(jnp.mean(o * o, axis=-1, keepdims=True) + EPS) * ow
            z = zb[r0 + c * CHUNK:r0 + (c + 1) * CHUNK, h * dk:(h + 1) * dk]
            o_ref[pl.ds(out_row + c * CHUNK, CHUNK), h * dk:(h + 1) * dk] = (
                (on * _silu(z)).astype(o_ref.dtype))
        fill()
    for h in heads:
        s_ref[h] = state[h]


def _gdn_kernel(h_ref, hn_ref, w_ref, cw_ref, al_ref, dt_ref, ow_ref, o_ref,
                xb0, xb1, zb0, zb1, hsel, s_ref):
    pt = C_PROJ_TILE
    ntile = h_ref.shape[0] // pt
    assert ntile % 2 == 0

    def project_steps(load_tile, halo, xb, zb):
        def chunk(c0, c1):
            def run():
                res = _dot(load_tile(), w_ref[:, c0:c1])
                if c0 < CCOL_Z:
                    xb[C_HALO:C_HALO + pt, c0:c1] = res
                else:
                    zb[:, c0 - CCOL_Z:c1 - CCOL_Z] = res
            return run

        bounds = list(range(0, CCOL_Z, C_PROJ_CHUNK)) + list(range(CCOL_Z, C_COLS, C_PROJ_CHUNK))
        ends = bounds[1:] + [C_COLS]
        return [halo] + [chunk(c0, c1) for c0, c1 in zip(bounds, ends)]

    def zero_halo(xb):
        def run():
            xb[0:C_HALO, :] = jnp.zeros((C_HALO, CCOL_Z), F32)
        return run

    def tile_rows(tile):
        return pl.ds(pl.multiple_of(tile * pt, pt), pt)

    def phase(steps, tile, xb, zb):
        units = [(r0, h0) for r0 in range(0, pt, C_SUB) for h0 in range(0, C_HEADS, C_HEAD_GROUP)]
        nslots = len(units) * (C_HEAD_GROUP * C_CONV_FILL + 8 + 2 * (C_SUB // CHUNK))
        pos = {"slot": 0, "step": 0}

        def fill(weight=1):
            pos["slot"] += weight
            target = min(len(steps), -(-pos["slot"] * len(steps) // nslots))
            while pos["step"] < target:
                steps[pos["step"]]()
                pos["step"] += 1

        for r0, h0 in units:
            out_row = pl.multiple_of(tile * pt + r0, C_SUB)
            _gdn_mix(range(h0, h0 + C_HEAD_GROUP), xb, zb, r0, out_row,
                     cw_ref, al_ref, dt_ref, ow_ref, o_ref, s_ref, fill)
        for step in steps[pos["step"]:]:
            step()

    s_ref[...] = jnp.zeros_like(s_ref)

    @pl.when(pl.program_id(0) == 0)
    def _():
        for step in project_steps(lambda: h_ref[tile_rows(0), :], zero_halo(xb0), xb0, zb0):
            step()

    def pair(j, carry):
        def halo1():
            xb1[0:C_HALO, :] = xb0[pt:pt + C_HALO, :]

        phase(project_steps(lambda: h_ref[tile_rows(2 * j + 1), :], halo1, xb1, zb1),
              2 * j, xb0, zb0)

        row_end = j == ntile // 2 - 1

        def halo0():
            hsel[...] = jnp.where(row_end, hn_ref[...],
                                  h_ref[tile_rows(jnp.minimum(2 * j + 2, ntile - 1)), :])
            xb0[0:C_HALO, :] = jnp.where(row_end, 0.0, xb1[pt:pt + C_HALO, :])

        phase(project_steps(lambda: hsel[...], halo0, xb0, zb0), 2 * j + 1, xb1, zb1)
        return carry

    lax.fori_loop(0, ntile // 2, pair, 0)


def _gated_deltanet(h3, w_c, conv_w, a_log, dt_bias, onorm_w):
    bsz, seq, d = h3.shape
    pt = C_PROJ_TILE
    pad = jnp.zeros((LANES - 2 * C_HEADS,), F32)
    zeros_h = jnp.zeros((C_HEADS,), F32)
    al_vec = jnp.concatenate([zeros_h, a_log.astype(F32), pad]).reshape(1, LANES)
    dt_vec = jnp.concatenate([zeros_h, dt_bias.astype(F32), pad]).reshape(1, LANES)
    vec_spec = pl.BlockSpec((1, LANES), lambda b: (0, 0))
    return pl.pallas_call(
        _gdn_kernel,
        grid=(bsz,),
        in_specs=[
            pl.BlockSpec((None, seq, d), lambda b: (b, 0, 0)),
            pl.BlockSpec((None, pt, d), lambda b: (jnp.minimum(b + 1, bsz - 1), 0, 0)),
            pl.BlockSpec((d, C_COLS), lambda b: (0, 0), pipeline_mode=pl.Buffered(1)),
            pl.BlockSpec((C_CONV_WIDTH, 3 * C_WIDTH), lambda b: (0, 0)),
            vec_spec, vec_spec, vec_spec,
        ],
        out_specs=pl.BlockSpec((None, seq, C_WIDTH), lambda b: (b, 0, 0)),
        out_shape=jax.ShapeDtypeStruct((bsz, seq, C_WIDTH), BF16),
        scratch_shapes=[pltpu.VMEM((C_HALO + pt, 3 * C_WIDTH), F32),
                        pltpu.VMEM((C_HALO + pt, 3 * C_WIDTH), F32),
                        pltpu.VMEM((pt, C_WIDTH + LANES), F32),
                        pltpu.VMEM((pt, C_WIDTH + LANES), F32),
                        pltpu.VMEM((pt, d), BF16),
                        pltpu.VMEM((C_HEADS, C_HEAD_DIM, C_HEAD_DIM), F32)],
        compiler_params=pltpu.CompilerParams(
            dimension_semantics=("arbitrary",), vmem_limit_bytes=VMEM_LIMIT),
        name="gated_deltanet",
    )(h3, h3, w_c, conv_w.astype(F32), al_vec, dt_vec, onorm_w.reshape(1, LANES).astype(F32))


def _outproj_kernel(x_ref, ya_ref, yb_ref, yc_ref, wa_ref, wb_ref, wc_ref, o_ref):
    acc = _dot(ya_ref[...], wa_ref[...])
    acc = acc + _dot(yb_ref[...], wb_ref[...])
    acc = acc + _dot(yc_ref[...], wc_ref[...])
    o_ref[...] = x_ref[...] + acc


def _out_proj(x2d, ya, yb, yc, w_out):
    n, d = x2d.shape
    tm = min(512, n)
    w16 = w_out.astype(BF16)
    wa, wb, wc = w16[:A_WIDTH], w16[A_WIDTH:A_WIDTH + B_WIDTH], w16[A_WIDTH + B_WIDTH:]
    row_spec = lambda width: pl.BlockSpec((tm, width), lambda i: (i, 0))
    w_spec = lambda width: pl.BlockSpec((width, d), lambda i: (0, 0))
    return pl.pallas_call(
        _outproj_kernel,
        grid=(n // tm,),
        in_specs=[row_spec(d), row_spec(A_WIDTH), row_spec(B_WIDTH), row_spec(C_WIDTH),
                  w_spec(A_WIDTH), w_spec(B_WIDTH), w_spec(C_WIDTH)],
        out_specs=row_spec(d),
        out_shape=jax.ShapeDtypeStruct((n, d), F32),
        compiler_params=pltpu.CompilerParams(
            dimension_semantics=("parallel",),
            vmem_limit_bytes=VMEM_LIMIT),
        name="out_proj",
    )(x2d, ya, yb, yc, wa, wb, wc)


_IN_WIDTHS = (A_WIDTH, A_KV_WIDTH, A_KV_WIDTH, A_WIDTH, 2 * B_WIDTH, B_WIDTH,
              3 * C_WIDTH, C_HEADS, C_HEADS, C_WIDTH)
_IN_NAMES = ("qa", "ka", "va", "za", "ub", "zb", "qkv_c", "b_c", "a_c", "zc")
_IN_START = {}
_off = 0
for _name, _width in zip(_IN_NAMES, _IN_WIDTHS):
    _IN_START[_name] = _off
    _off += _width
IN_COLS = _off
_A_LAYOUT = ((COL_QA, "qa", A_WIDTH), (COL_ZA, "za", A_WIDTH), (COL_KVA, "ka", A_KV_WIDTH),
             (COL_KVA + A_KV_WIDTH, "va", A_KV_WIDTH))
_B_LAYOUT = ((0, "ub", 2 * B_WIDTH), (BCOL_Z, "zb", B_WIDTH))
_C_LAYOUT = ((0, "qkv_c", 3 * C_WIDTH), (CCOL_Z, "zc", C_WIDTH))


def _prep_kernel(w_ref, a_ref, b_ref, c_ref):
    for out_ref, layout in ((a_ref, _A_LAYOUT), (b_ref, _B_LAYOUT), (c_ref, _C_LAYOUT)):
        for dst, name, width in layout:
            src = _IN_START[name]
            if src % LANES == 0:
                out_ref[:, dst:dst + width] = w_ref[:, src:src + width].astype(BF16)
            else:
                base = src - src % LANES
                assert src + width == IN_COLS
                tail = w_ref[:, base:IN_COLS]
                out_ref[:, dst:dst + width] = tail[:, src - base:].astype(BF16)
    b0 = _IN_START["b_c"]
    assert b0 % LANES == 0 and b0 + LANES <= IN_COLS
    lane = lax.broadcasted_iota(jnp.int32, (w_ref.shape[0], LANES), 1)
    c_ref[:, CCOL_BA:CCOL_BA + LANES] = jnp.where(
        lane < 2 * C_HEADS, w_ref[:, b0:b0 + LANES], 0.0).astype(BF16)


def _prep_w_in(w_in, layer):
    _, d, cols = w_in.shape
    tr = 256
    widths = (A_COLS, B_COLS, C_COLS)
    return pl.pallas_call(
        _prep_kernel,
        grid=(d // tr,),
        in_specs=[pl.BlockSpec((None, tr, cols), lambda i: (layer, i, 0))],
        out_specs=[pl.BlockSpec((tr, width), lambda i: (i, 0)) for width in widths],
        out_shape=[jax.ShapeDtypeStruct((d, width), BF16) for width in widths],
        compiler_params=pltpu.CompilerParams(
            dimension_semantics=("parallel",), vmem_limit_bytes=VMEM_LIMIT),
        name="prep_w_in",
    )(w_in)


def _layer_parts(l, x2d, bsz, seq, prm, tabs=None):
    d = x2d.shape[1]
    if tabs is None:
        tabs = _rope_tables(prm["positions"])
    w_a, w_b, w_c = _prep_w_in(prm["w_in"], l)
    p2d, h2d = _in_proj(x2d, prm["norm_w"][l], w_a)
    p3 = p2d.reshape(bsz, seq, A_COLS)
    h3 = h2d.reshape(bsz, seq, d)
    ya = _attention(p3, tabs, prm["q_norm_w"][l], prm["k_norm_w"][l], prm["sinks"][l])
    yb = _conv_module(h3, w_b, prm["b_conv_w"][l], prm["b_conv_b"][l], prm["b_ln_w"][l],
                      prm["b_ln_b"][l], prm["b_pw_w"][l], prm["b_pw_b"][l])
    yc = _gated_deltanet(h3, w_c, prm["c_conv_w"][l], prm["c_a_log"][l],
                         prm["c_dt_bias"][l], prm["c_onorm_w"][l])
    xo = _out_proj(x2d, ya.reshape(bsz * seq, A_WIDTH), yb.reshape(bsz * seq, B_WIDTH),
                   yc.reshape(bsz * seq, C_WIDTH), prm["w_out"][l])
    return {"ya": ya, "yb": yb, "yc": yc, "xo": xo}


def kernel(x, positions, norm_w, w_in, q_norm_w, k_norm_w, sinks, b_conv_w, b_conv_b,
           b_ln_w, b_ln_b, b_pw_w, b_pw_b, c_conv_w, c_a_log, c_dt_bias, c_onorm_w, w_out):
    bsz, seq, d = x.shape
    prm = dict(positions=positions, norm_w=norm_w, w_in=w_in, q_norm_w=q_norm_w,
               k_norm_w=k_norm_w, sinks=sinks, b_conv_w=b_conv_w, b_conv_b=b_conv_b,
               b_ln_w=b_ln_w, b_ln_b=b_ln_b, b_pw_w=b_pw_w, b_pw_b=b_pw_b, c_conv_w=c_conv_w,
               c_a_log=c_a_log, c_dt_bias=c_dt_bias, c_onorm_w=c_onorm_w, w_out=w_out)
    tabs = _rope_tables(positions)
    x2d = x.reshape(bsz * seq, d)
    for l in range(w_in.shape[0]):
        x2d = _layer_parts(l, x2d, bsz, seq, prm, tabs)["xo"]
    return x2d.reshape(bsz, seq, d)
```

```python
import jax
import jax.numpy as jnp
from jax import lax
from jax.experimental import pallas as pl
from jax.experimental.pallas import tpu as pltpu

F32 = jnp.float32
BF16 = jnp.bfloat16

EPS = 1e-6
ROPE_THETA = 500000.0
A_HEAD_DIM = 64
A_Q_HEADS = 12
A_KV_HEADS = 4
A_GROUP = 3
A_WIDTH = A_Q_HEADS * A_HEAD_DIM
A_KV_WIDTH = A_KV_HEADS * A_HEAD_DIM
ATTN_BLOCK = 128
ROT_DIM = 16
B_WIDTH = 512
B_CONV_WIDTH = 31
C_HEAD_DIM = 128
C_HEADS = 6
C_WIDTH = C_HEADS * C_HEAD_DIM
C_CONV_WIDTH = 4
CHUNK = 64
CHUNK_SHIFT = CHUNK.bit_length() - 1
LANES = 128
SUBLANES = 8

COL_QA = 0
COL_ZA = COL_QA + A_WIDTH
COL_KVA = COL_ZA + A_WIDTH
A_COLS = COL_KVA + 2 * A_KV_WIDTH
BCOL_GATE = B_WIDTH
BCOL_Z = 2 * B_WIDTH
B_COLS = BCOL_Z + B_WIDTH
CCOL_Z = 3 * C_WIDTH
CCOL_BA = CCOL_Z + C_WIDTH
C_COLS = CCOL_BA + LANES

NEG_BIG = -1e30
VMEM_LIMIT = 56 * 1024 * 1024


def _sigmoid(x):
    return 1.0 / (1.0 + jnp.exp(-x))


def _silu(x):
    return x * _sigmoid(x)


def _dot(a, b):
    return jnp.dot(a, b, preferred_element_type=F32)


def _dot_nt(a, b):
    return lax.dot_general(a, b, (((1,), (1,)), ((), ())), preferred_element_type=F32)


def _rope_kernel(pos_ref, invf_ref, cos_ref, sin_ref):
    p = pos_ref[...].astype(F32)
    for f in range(ROT_DIM // 2):
        ang = p * invf_ref[f:f + 1, :]
        cos_ref[f] = jnp.cos(ang)
        sin_ref[f] = jnp.sin(ang)


def _rope_tables(positions):
    bsz, seq = positions.shape
    rows = bsz * seq // LANES
    nf = ROT_DIM // 2
    inv_freq = ROPE_THETA ** (-jnp.arange(0, ROT_DIM, 2, dtype=F32) / ROT_DIM)
    invf = jnp.broadcast_to(inv_freq[:, None], (nf, LANES))
    cos, sin = pl.pallas_call(
        _rope_kernel,
        out_shape=[jax.ShapeDtypeStruct((nf, rows, LANES), F32)] * 2,
    )(positions.reshape(rows, LANES), invf)
    cos = cos.reshape(nf, bsz, seq).transpose(1, 2, 0)
    sin = sin.reshape(nf, bsz, seq).transpose(1, 2, 0)
    return jnp.concatenate([cos, sin, jnp.zeros((bsz, seq, LANES - ROT_DIM), F32)], axis=-1)


NORM_ROW_CHUNK = 256


def _inproj_kernel(x_ref, nw_ref, w_ref, p_ref, h_ref):
    nw = nw_ref[...]
    chunk = min(NORM_ROW_CHUNK, x_ref.shape[0])
    for r0 in range(0, x_ref.shape[0], chunk):
        x = x_ref[r0:r0 + chunk, :]
        ms = jnp.mean(x * x, axis=-1, keepdims=True)
        h = (x * lax.rsqrt(ms + EPS) * nw).astype(h_ref.dtype)
        h_ref[r0:r0 + chunk, :] = h
        p_ref[r0:r0 + chunk, :] = _dot(h, w_ref[...]).astype(p_ref.dtype)


def _in_proj(x2d, norm_w, w):
    n, d = x2d.shape
    cols = w.shape[1]
    tm = min(1024, n)
    return pl.pallas_call(
        _inproj_kernel,
        grid=(n // tm,),
        in_specs=[pl.BlockSpec((tm, d), lambda i: (i, 0)),
                  pl.BlockSpec((1, d), lambda i: (0, 0)),
                  pl.BlockSpec((d, cols), lambda i: (0, 0))],
        out_specs=[pl.BlockSpec((tm, cols), lambda i: (i, 0)),
                   pl.BlockSpec((tm, d), lambda i: (i, 0))],
        out_shape=[jax.ShapeDtypeStruct((n, cols), BF16),
                   jax.ShapeDtypeStruct((n, d), BF16)],
        compiler_params=pltpu.CompilerParams(
            dimension_semantics=("parallel",), vmem_limit_bytes=VMEM_LIMIT),
        name="in_proj",
    )(x2d, norm_w.reshape(1, d).astype(F32), w)


def _attn_kernel(sink_ref, q_ref, za_ref, kv_ref, tab_ref, qw_ref, kw_ref,
                 o_ref, kp_ref, vp_ref):
    n = pl.program_id(1)
    blk = ATTN_BLOCK
    hd = A_HEAD_DIM
    half = ROT_DIM // 2

    @pl.when(n == 0)
    def _():
        kp_ref[...] = jnp.zeros_like(kp_ref)
        vp_ref[...] = jnp.zeros_like(vp_ref)

    lane = lax.broadcasted_iota(jnp.int32, (blk, LANES), 1)
    row = lax.broadcasted_iota(jnp.int32, (blk, LANES), 0)
    lo = lane < hd

    tab = tab_ref[...]
    in_head = lane & (hd - 1)
    first = in_head < half
    second = (in_head >= half) & (in_head < ROT_DIM)
    cos_l = jnp.where(first, tab, pltpu.roll(tab, half, 1))
    cos_h = jnp.where(first, pltpu.roll(tab, hd, 1), pltpu.roll(tab, hd + half, 1))
    c_tab = jnp.where(in_head < ROT_DIM, jnp.where(lo, cos_l, cos_h), 1.0)
    s1_tab = jnp.where(first, -jnp.where(lo, pltpu.roll(tab, LANES - half, 1),
                                         pltpu.roll(tab, hd - half, 1)), 0.0)
    s2_tab = jnp.where(second, jnp.where(lo, tab, pltpu.roll(tab, hd, 1)), 0.0)

    def norm_rope(xs, ws):
        sq = [x * x for x in xs]
        s_lo = [jnp.sum(jnp.where(lo, ss, 0.0), axis=-1, keepdims=True) for ss in sq]
        s_hi = [jnp.sum(jnp.where(lo, 0.0, ss), axis=-1, keepdims=True) for ss in sq]
        ys = [x * lax.rsqrt(jnp.where(lo, a, b) * (1.0 / hd) + EPS) * w
              for x, a, b, w in zip(xs, s_lo, s_hi, ws)]
        up = [pltpu.roll(y, LANES - ROT_DIM // 2, 1) for y in ys]
        down = [pltpu.roll(y, ROT_DIM // 2, 1) for y in ys]
        return [y * c_tab + u * s1_tab + d * s2_tab for y, u, d in zip(ys, up, down)]

    def both_halves(x):
        sw = pltpu.roll(x, hd, 1)
        return jnp.where(lo, x, sw), jnp.where(lo, sw, x)

    n_kpair = A_KV_WIDTH // LANES
    n_qpair = A_WIDTH // LANES
    roped = norm_rope(
        [kv_ref[:, kc * LANES:(kc + 1) * LANES].astype(F32) for kc in range(n_kpair)]
        + [q_ref[:, c * LANES:(c + 1) * LANES].astype(F32) for c in range(n_qpair)],
        [kw_ref[...]] * n_kpair + [qw_ref[...]] * n_qpair)
    k_cat, v_cat = [], []
    for kc in range(n_kpair):
        v_pair = kv_ref[:, A_KV_WIDTH + kc * LANES:A_KV_WIDTH + (kc + 1) * LANES].astype(F32)
        for hh, (k_dup, v_dup) in enumerate(zip(both_halves(roped[kc]), both_halves(v_pair))):
            g = 2 * kc + hh
            k_dup = k_dup.astype(BF16)
            v_dup = v_dup.astype(BF16)
            k_cat.append(jnp.concatenate([kp_ref[g], k_dup], axis=0))
            v_cat.append(jnp.concatenate([vp_ref[g], v_dup], axis=0))
            kp_ref[g] = k_dup
            vp_ref[g] = v_dup

    q_pairs = [(y * (hd ** -0.5)).astype(BF16) for y in roped[n_kpair:]]
    zero16 = jnp.zeros((blk, LANES), BF16)

    upper = lane > row
    prev_bias = jnp.where(n > 0, 0.0, NEG_BIG)

    groups = [range(g * A_GROUP, (g + 1) * A_GROUP) for g in range(A_KV_HEADS)]
    s_all = []
    for g, heads in enumerate(groups):
        q_stack = jnp.concatenate(
            [jnp.where(lo, q_pairs[h // 2], zero16) if h % 2 == 0
             else jnp.where(lo, zero16, q_pairs[h // 2]) for h in heads], axis=0)
        s_all.append(_dot_nt(q_stack, k_cat[g]))
    p_all, dens = [], [None] * A_Q_HEADS
    for g, heads in enumerate(groups):
        p_rows = []
        for r, h in enumerate(heads):
            s_r = s_all[g][r * blk:(r + 1) * blk]
            s = jnp.where(upper, s_r[:, :blk] + prev_bias, s_r[:, blk:])
            sink = sink_ref[h]
            m = jnp.maximum(jnp.max(s, axis=-1, keepdims=True), sink)
            p = jnp.exp(s - m)
            dens[h] = jnp.sum(p, axis=-1, keepdims=True) + jnp.exp(sink - m)
            p16 = p.astype(BF16)
            p_rows.append(jnp.concatenate([jnp.where(upper, p16, zero16),
                                           jnp.where(upper, zero16, p16)], axis=-1))
        p_all.append(jnp.concatenate(p_rows, axis=0))
    head_out = [None] * A_Q_HEADS
    for g, heads in enumerate(groups):
        o_all = _dot(p_all[g], v_cat[g])
        for r, h in enumerate(heads):
            head_out[h] = o_all[r * blk:(r + 1) * blk] / dens[h]

    for c in range(A_WIDTH // LANES):
        z = za_ref[:, c * LANES:(c + 1) * LANES].astype(F32)
        o_pair = jnp.where(lo, head_out[2 * c], head_out[2 * c + 1])
        o_ref[:, c * LANES:(c + 1) * LANES] = (o_pair * _silu(z)).astype(o_ref.dtype)


def _attention(p3, tabs, q_norm_w, k_norm_w, sinks):
    bsz, seq, _ = p3.shape
    blk = ATTN_BLOCK
    tile2 = lambda w: jnp.concatenate([w, w]).reshape(1, LANES).astype(F32)
    tab_spec = pl.BlockSpec((None, blk, LANES), lambda b, n: (b, n, 0))
    vec_spec = pl.BlockSpec((1, LANES), lambda b, n: (0, 0))
    return pl.pallas_call(
        _attn_kernel,
        grid=(bsz, seq // blk),
        in_specs=[
            pl.BlockSpec(memory_space=pltpu.SMEM),
            pl.BlockSpec((None, blk, A_WIDTH), lambda b, n: (b, n, COL_QA // A_WIDTH)),
            pl.BlockSpec((None, blk, A_WIDTH), lambda b, n: (b, n, COL_ZA // A_WIDTH)),
            pl.BlockSpec((None, blk, 2 * A_KV_WIDTH),
                         lambda b, n: (b, n, COL_KVA // (2 * A_KV_WIDTH))),
            tab_spec, vec_spec, vec_spec,
        ],
        out_specs=pl.BlockSpec((None, blk, A_WIDTH), lambda b, n: (b, n, 0)),
        out_shape=jax.ShapeDtypeStruct((bsz, seq, A_WIDTH), BF16),
        scratch_shapes=[pltpu.VMEM((A_KV_HEADS, blk, LANES), BF16),
                        pltpu.VMEM((A_KV_HEADS, blk, LANES), BF16)],
        compiler_params=pltpu.CompilerParams(
            dimension_semantics=("parallel", "arbitrary")),
        name="swa_attention",
    )(sinks.astype(F32), p3, p3, p3, tabs, tile2(q_norm_w), tile2(k_norm_w))


B_HALO = 32
B_ROW_CHUNK = 64
B_PROJ_TILE = 256
B_PROJ_CHUNK = 256
B_TAP_GROUP = 16


def _convb_mix(hb, zb, out_row0, cw_ref, cb_ref, lnw_ref, lnb_ref, pw_ref, pb_ref, o_ref, fill):
    cb = cb_ref[...]
    lnw = lnw_ref[...]
    lnb = lnb_ref[...]
    pb = pb_ref[...]
    first = B_HALO - (B_CONV_WIDTH - 1)
    chunks = range(0, zb.shape[0], B_ROW_CHUNK)
    accs = []
    for r0 in chunks:
        acc = jnp.broadcast_to(cb, (B_ROW_CHUNK, B_WIDTH))
        for k in range(B_CONV_WIDTH):
            shift = (first + k) % SUBLANES
            base = r0 + first + k - shift
            acc = acc + cw_ref[k:k + 1, :] * hb[shift, base:base + B_ROW_CHUNK, :]
            if k % B_TAP_GROUP == B_TAP_GROUP - 1:
                fill()
        accs.append(acc)
        fill()
    mus = [jnp.mean(acc, axis=-1, keepdims=True) for acc in accs]
    cens = [acc - mu for acc, mu in zip(accs, mus)]
    varis = [jnp.mean(cen * cen, axis=-1, keepdims=True) for cen in cens]
    ys = [_silu(cen * lax.rsqrt(var + EPS) * lnw + lnb).astype(BF16)
          for cen, var in zip(cens, varis)]
    outs = [_dot(y, pw_ref[...]) + pb for y in ys]
    for r0, out in zip(chunks, outs):
        z = zb[r0:r0 + B_ROW_CHUNK, :]
        o_ref[pl.ds(out_row0 + r0, B_ROW_CHUNK), :] = (out * _silu(z)).astype(o_ref.dtype)


def _convb_kernel(h_ref, w_ref, cw_ref, cb_ref, lnw_ref, lnb_ref, pw_ref, pb_ref, o_ref,
                  hb0, hb1, zb0, zb1):
    pt = B_PROJ_TILE
    ntile = h_ref.shape[0] // pt
    assert ntile % 2 == 0
    span = B_HALO + pt - SUBLANES

    def project_steps(tile, hb, zb, hb_prev):
        rows = pl.ds(pl.multiple_of(tile * pt, pt), pt)

        def halo():
            if hb_prev is None:
                hb[0, 0:B_HALO, :] = jnp.zeros((B_HALO, B_WIDTH), F32)
            else:
                hb[0, 0:B_HALO, :] = hb_prev[0, pt:pt + B_HALO, :]

        def glu(c0):
            def run():
                hh = h_ref[rows, :]
                a = _dot(hh, w_ref[:, c0:c0 + B_PROJ_CHUNK])
                g = _dot(hh, w_ref[:, BCOL_GATE + c0:BCOL_GATE + c0 + B_PROJ_CHUNK])
                hb[0, B_HALO:B_HALO + pt, c0:c0 + B_PROJ_CHUNK] = a * _sigmoid(g)
            return run

        def gate(c0):
            def run():
                zb[:, c0:c0 + B_PROJ_CHUNK] = _dot(
                    h_ref[rows, :], w_ref[:, BCOL_Z + c0:BCOL_Z + c0 + B_PROJ_CHUNK])
            return run

        def shifted(r):
            def run():
                hb[r, 0:span, :] = hb[0, r:r + span, :]
            return run

        cols = range(0, B_WIDTH, B_PROJ_CHUNK)
        return ([halo] + [glu(c0) for c0 in cols] + [gate(c0) for c0 in cols]
                + [shifted(r) for r in range(1, SUBLANES)])

    def phase(steps, tile, hb, zb):
        nslots = (pt // B_ROW_CHUNK) * (B_CONV_WIDTH // B_TAP_GROUP + 1)
        pos = {"slot": 0, "step": 0}

        def fill():
            pos["slot"] += 1
            target = min(len(steps), -(-pos["slot"] * len(steps) // nslots))
            while pos["step"] < target:
                steps[pos["step"]]()
                pos["step"] += 1

        _convb_mix(hb, zb, pl.multiple_of(tile * pt, pt), cw_ref, cb_ref, lnw_ref, lnb_ref,
                   pw_ref, pb_ref, o_ref, fill)
        for step in steps[pos["step"]:]:
            step()

    for step in project_steps(0, hb0, zb0, None):
        step()

    def pair(j, carry):
        phase(project_steps(2 * j + 1, hb1, zb1, hb0), 2 * j, hb0, zb0)
        phase(project_steps(jnp.minimum(2 * j + 2, ntile - 1), hb0, zb0, hb1), 2 * j + 1, hb1, zb1)
        return carry

    lax.fori_loop(0, ntile // 2, pair, 0)


def _conv_module(h3, w_b, conv_w, conv_b, ln_w, ln_b, pw_w, pw_b):
    bsz, seq, d = h3.shape
    pt = B_PROJ_TILE
    row = lambda v: v.reshape(1, B_WIDTH).astype(F32)
    vec_spec = pl.BlockSpec((1, B_WIDTH), lambda b: (0, 0))
    return pl.pallas_call(
        _convb_kernel,
        grid=(bsz,),
        in_specs=[
            pl.BlockSpec((None, seq, d), lambda b: (b, 0, 0)),
            pl.BlockSpec((d, B_COLS), lambda b: (0, 0)),
            pl.BlockSpec((B_CONV_WIDTH, B_WIDTH), lambda b: (0, 0)),
            vec_spec, vec_spec, vec_spec,
            pl.BlockSpec((B_WIDTH, B_WIDTH), lambda b: (0, 0)),
            vec_spec,
        ],
        out_specs=pl.BlockSpec((None, seq, B_WIDTH), lambda b: (b, 0, 0)),
        out_shape=jax.ShapeDtypeStruct((bsz, seq, B_WIDTH), BF16),
        scratch_shapes=[pltpu.VMEM((SUBLANES, pt + B_HALO, B_WIDTH), F32),
                        pltpu.VMEM((SUBLANES, pt + B_HALO, B_WIDTH), F32),
                        pltpu.VMEM((pt, B_WIDTH), F32),
                        pltpu.VMEM((pt, B_WIDTH), F32)],
        compiler_params=pltpu.CompilerParams(
            dimension_semantics=("parallel",), vmem_limit_bytes=VMEM_LIMIT),
        name="conformer_conv",
    )(h3, w_b, conv_w.astype(F32), row(conv_b), row(ln_w), row(ln_b),
      pw_w.astype(BF16), row(pw_b))


C_SUB = 2 * CHUNK
C_PROJ_TILE = 256
C_PROJ_CHUNK = 256
C_CONV_FILL = 1
C_HALO = 8
C_HEAD_GROUP = 6


def _gdn_mix(heads, xb, zb, r0, out_row, cw_ref, al_ref, dt_ref, ow_ref, o_ref, s_ref, fill):
    ts = C_SUB
    dk = C_HEAD_DIM
    nchunk = ts // CHUNK

    def conv_silu(col0):
        cols = slice(col0, col0 + dk)
        first = C_HALO + r0 - (C_CONV_WIDTH - 1)
        acc = cw_ref[0:1, cols] * xb[first:first + ts, cols]
        for k in range(1, C_CONV_WIDTH):
            acc = acc + cw_ref[k:k + 1, cols] * xb[first + k:first + k + ts, cols]
        return _silu(acc)

    def l2n(x):
        return x * lax.rsqrt(jnp.sum(x * x, axis=-1, keepdims=True) + EPS)

    lane = lax.broadcasted_iota(jnp.int32, (ts, LANES), 1)
    row = lax.broadcasted_iota(jnp.int32, (ts, LANES), 0)
    ba = zb[r0:r0 + ts, C_WIDTH:C_WIDTH + LANES]
    beta = _sigmoid(ba)
    sp_in = ba + dt_ref[...]
    softplus = jnp.maximum(sp_in, 0.0) + jnp.log1p(jnp.exp(-jnp.abs(sp_in)))
    decay_lanes = (lane >= C_HEADS) & (lane < 2 * C_HEADS)
    g_log = jnp.where(decay_lanes, -jnp.exp(al_ref[...]) * softplus, 0.0)

    in_chunk = row & (CHUNK - 1)
    g_cum = g_log
    sh = 1
    while sh < CHUNK:
        g_cum = g_cum + jnp.where(in_chunk >= sh, pltpu.roll(g_cum, sh, 0), 0.0)
        sh *= 2
    g_cum_t = g_cum.T
    g_end = jnp.concatenate(
        [jnp.broadcast_to(g_cum[(c + 1) * CHUNK - 1:(c + 1) * CHUNK, :], (CHUNK, LANES))
         for c in range(nchunk)], axis=0)

    same_chunk = (row >> CHUNK_SHIFT) == (lane >> CHUNK_SHIFT)
    incl = same_chunk & (row >= lane)
    strict = same_chunk & (row > lane)
    eye = jnp.where(row == lane, 1.0, 0.0).astype(F32)
    ow = ow_ref[...]

    q, k, v, gcol, xp, inv, intra, rhs, kdec_t, lhs_s = ({} for _ in range(10))
    for h in heads:
        q[h] = l2n(conv_silu(h * dk)) * (dk ** -0.5)
        k[h] = l2n(conv_silu(C_WIDTH + h * dk))
        v[h] = conv_silu(2 * C_WIDTH + h * dk)
        fill(C_CONV_FILL)
    for h in heads:
        gcol[h] = g_cum[:, C_HEADS + h:C_HEADS + h + 1]
        grow = g_cum_t[C_HEADS + h:C_HEADS + h + 1, :]
        bcol = beta[:, h:h + 1]
        decay = jnp.exp(jnp.where(incl, gcol[h] - grow, NEG_BIG))
        kb = k[h] * bcol
        egc = jnp.exp(gcol[h])
        k16 = k[h].astype(BF16)
        kk_qk = _dot_nt(jnp.concatenate([kb.astype(BF16), q[h].astype(BF16)], axis=0), k16)
        xp[h] = -jnp.where(strict, kk_qk[:ts] * decay, 0.0)
        intra[h] = (kk_qk[ts:] * decay).astype(BF16)
        rhs[h] = jnp.concatenate([(v[h] * bcol).astype(BF16), (kb * egc).astype(BF16)], axis=-1)
        kdec_t[h] = (k[h] * jnp.exp(g_end[:, C_HEADS + h:C_HEADS + h + 1] - gcol[h])).T.astype(BF16)
        lhs_s[h] = (q[h] * egc).astype(BF16)
    fill()

    for h in heads:
        inv[h] = eye + xp[h]
        x16 = xp[h].astype(BF16)
        xp[h] = _dot(x16, x16)
    fill()
    for level in range(1, 5):
        for h in heads:
            x16 = xp[h].astype(BF16)
            both = _dot(jnp.concatenate([x16, inv[h].astype(BF16)], axis=0), x16)
            xp[h] = both[:ts]
            inv[h] = inv[h] + both[ts:]
        fill()
    u, w = {}, {}
    for h in heads:
        inv[h] = inv[h] + _dot(inv[h].astype(BF16), xp[h].astype(BF16))
    fill()
    for h in heads:
        uw = _dot(inv[h].astype(BF16), rhs[h])
        u[h] = uw[:, :dk]
        w[h] = uw[:, dk:].astype(BF16)
    fill()

    state = {h: s_ref[h] for h in heads}
    zeros_c = jnp.zeros((CHUNK, dk), BF16)
    for c in range(nchunk):
        rows = slice(c * CHUNK, (c + 1) * CHUNK)
        v_pad, o_inter = {}, {}
        for h in heads:
            ws_qs = _dot(jnp.concatenate([w[h][rows], lhs_s[h][rows]], axis=0),
                         state[h].astype(BF16))
            v_new = (u[h][rows] - ws_qs[:CHUNK]).astype(BF16)
            o_inter[h] = ws_qs[CHUNK:]
            v_pad[h] = jnp.concatenate([v_new if cc == c else zeros_c for cc in range(nchunk)], axis=0)
        fill()
        for h in heads:
            av_kv = _dot(jnp.concatenate([intra[h][rows], kdec_t[h]], axis=0), v_pad[h])
            o = o_inter[h] + av_kv[:CHUNK]
            g_last = gcol[h][(c + 1) * CHUNK - 1:(c + 1) * CHUNK, :]
            state[h] = state[h] * jnp.exp(g_last) + av_kv[CHUNK:]
            on = o * lax.rsqrt(jnp.mean(o * o, axis=-1, keepdims=True) + EPS) * ow
            z = zb[r0 + c * CHUNK:r0 + (c + 1) * CHUNK, h * dk:(h + 1) * dk]
            o_ref[pl.ds(out_row + c * CHUNK, CHUNK), h * dk:(h + 1) * dk] = (
                (on * _silu(z)).astype(o_ref.dtype))
        fill()
    for h in heads:
        s_ref[h] = state[h]


def _gdn_kernel(h_ref, hn_ref, w_ref, cw_ref, al_ref, dt_ref, ow_ref, o_ref,
                xb0, xb1, zb0, zb1, hsel, s_ref):
    pt = C_PROJ_TILE
    ntile = h_ref.shape[0] // pt
    assert ntile % 2 == 0

    def project_steps(load_tile, halo, xb, zb):
        def chunk(c0, c1):
            def run():
                res = _dot(load_tile(), w_ref[:, c0:c1])
                if c0 < CCOL_Z:
                    xb[C_HALO:C_HALO + pt, c0:c1] = res
                else:
                    zb[:, c0 - CCOL_Z:c1 - CCOL_Z] = res
            return run

        bounds = list(range(0, CCOL_Z, C_PROJ_CHUNK)) + list(range(CCOL_Z, C_COLS, C_PROJ_CHUNK))
        ends = bounds[1:] + [C_COLS]
        return [halo] + [chunk(c0, c1) for c0, c1 in zip(bounds, ends)]

    def zero_halo(xb):
        def run():
            xb[0:C_HALO, :] = jnp.zeros((C_HALO, CCOL_Z), F32)
        return run

    def tile_rows(tile):
        return pl.ds(pl.multiple_of(tile * pt, pt), pt)

    def phase(steps, tile, xb, zb):
        units = [(r0, h0) for r0 in range(0, pt, C_SUB) for h0 in range(0, C_HEADS, C_HEAD_GROUP)]
        nslots = len(units) * (C_HEAD_GROUP * C_CONV_FILL + 8 + 2 * (C_SUB // CHUNK))
        pos = {"slot": 0, "step": 0}

        def fill(weight=1):
            pos["slot"] += weight
            target = min(len(steps), -(-pos["slot"] * len(steps) // nslots))
            while pos["step"] < target:
                steps[pos["step"]]()
                pos["step"] += 1

        for r0, h0 in units:
            out_row = pl.multiple_of(tile * pt + r0, C_SUB)
            _gdn_mix(range(h0, h0 + C_HEAD_GROUP), xb, zb, r0, out_row,
                     cw_ref, al_ref, dt_ref, ow_ref, o_ref, s_ref, fill)
        for step in steps[pos["step"]:]:
            step()

    s_ref[...] = jnp.zeros_like(s_ref)

    @pl.when(pl.program_id(0) == 0)
    def _():
        for step in project_steps(lambda: h_ref[tile_rows(0), :], zero_halo(xb0), xb0, zb0):
            step()

    def pair(j, carry):
        def halo1():
            xb1[0:C_HALO, :] = xb0[pt:pt + C_HALO, :]

        phase(project_steps(lambda: h_ref[tile_rows(2 * j + 1), :], halo1, xb1, zb1),
              2 * j, xb0, zb0)

        row_end = j == ntile // 2 - 1

        def halo0():
            hsel[...] = jnp.where(row_end, hn_ref[...],
                                  h_ref[tile_rows(jnp.minimum(2 * j + 2, ntile - 1)), :])
            xb0[0:C_HALO, :] = jnp.where(row_end, 0.0, xb1[pt:pt + C_HALO, :])

        phase(project_steps(lambda: hsel[...], halo0, xb0, zb0), 2 * j + 1, xb1, zb1)
        return carry

    lax.fori_loop(0, ntile // 2, pair, 0)


def _gated_deltanet(h3, w_c, conv_w, a_log, dt_bias, onorm_w):
    bsz, seq, d = h3.shape
    pt = C_PROJ_TILE
    pad = jnp.zeros((LANES - 2 * C_HEADS,), F32)
    zeros_h = jnp.zeros((C_HEADS,), F32)
    al_vec = jnp.concatenate([zeros_h, a_log.astype(F32), pad]).reshape(1, LANES)
    dt_vec = jnp.concatenate([zeros_h, dt_bias.astype(F32), pad]).reshape(1, LANES)
    vec_spec = pl.BlockSpec((1, LANES), lambda b: (0, 0))
    return pl.pallas_call(
        _gdn_kernel,
        grid=(bsz,),
        in_specs=[
            pl.BlockSpec((None, seq, d), lambda b: (b, 0, 0)),
            pl.BlockSpec((None, pt, d), lambda b: (jnp.minimum(b + 1, bsz - 1), 0, 0)),
            pl.BlockSpec((d, C_COLS), lambda b: (0, 0), pipeline_mode=pl.Buffered(1)),
            pl.BlockSpec((C_CONV_WIDTH, 3 * C_WIDTH), lambda b: (0, 0)),
            vec_spec, vec_spec, vec_spec,
        ],
        out_specs=pl.BlockSpec((None, seq, C_WIDTH), lambda b: (b, 0, 0)),
        out_shape=jax.ShapeDtypeStruct((bsz, seq, C_WIDTH), BF16),
        scratch_shapes=[pltpu.VMEM((C_HALO + pt, 3 * C_WIDTH), F32),
                        pltpu.VMEM((C_HALO + pt, 3 * C_WIDTH), F32),
                        pltpu.VMEM((pt, C_WIDTH + LANES), F32),
                        pltpu.VMEM((pt, C_WIDTH + LANES), F32),
                        pltpu.VMEM((pt, d), BF16),
                        pltpu.VMEM((C_HEADS, C_HEAD_DIM, C_HEAD_DIM), F32)],
        compiler_params=pltpu.CompilerParams(
            dimension_semantics=("arbitrary",), vmem_limit_bytes=VMEM_LIMIT),
        name="gated_deltanet",
    )(h3, h3, w_c, conv_w.astype(F32), al_vec, dt_vec, onorm_w.reshape(1, LANES).astype(F32))


def _outproj_kernel(x_ref, ya_ref, yb_ref, yc_ref, wa_ref, wb_ref, wc_ref, o_ref):
    acc = _dot(ya_ref[...], wa_ref[...])
    acc = acc + _dot(yb_ref[...], wb_ref[...])
    acc = acc + _dot(yc_ref[...], wc_ref[...])
    o_ref[...] = x_ref[...] + acc


def _out_proj(x2d, ya, yb, yc, w_out):
    n, d = x2d.shape
    tm = min(512, n)
    w16 = w_out.astype(BF16)
    wa, wb, wc = w16[:A_WIDTH], w16[A_WIDTH:A_WIDTH + B_WIDTH], w16[A_WIDTH + B_WIDTH:]
    row_spec = lambda width: pl.BlockSpec((tm, width), lambda i: (i, 0))
    w_spec = lambda width: pl.BlockSpec((width, d), lambda i: (0, 0))
    return pl.pallas_call(
        _outproj_kernel,
        grid=(n // tm,),
        in_specs=[row_spec(d), row_spec(A_WIDTH), row_spec(B_WIDTH), row_spec(C_WIDTH),
                  w_spec(A_WIDTH), w_spec(B_WIDTH), w_spec(C_WIDTH)],
        out_specs=row_spec(d),
        out_shape=jax.ShapeDtypeStruct((n, d), F32),
        compiler_params=pltpu.CompilerParams(
            dimension_semantics=("parallel",),
            vmem_limit_bytes=VMEM_LIMIT),
        name="out_proj",
    )(x2d, ya, yb, yc, wa, wb, wc)


_IN_WIDTHS = (A_WIDTH, A_KV_WIDTH, A_KV_WIDTH, A_WIDTH, 2 * B_WIDTH, B_WIDTH,
              3 * C_WIDTH, C_HEADS, C_HEADS, C_WIDTH)
_IN_NAMES = ("qa", "ka", "va", "za", "ub", "zb", "qkv_c", "b_c", "a_c", "zc")
_IN_START = {}
_off = 0
for _name, _width in zip(_IN_NAMES, _IN_WIDTHS):
    _IN_START[_name] = _off
    _off += _width
IN_COLS = _off
_A_LAYOUT = ((COL_QA, "qa", A_WIDTH), (COL_ZA, "za", A_WIDTH), (COL_KVA, "ka", A_KV_WIDTH),
             (COL_KVA + A_KV_WIDTH, "va", A_KV_WIDTH))
_B_LAYOUT = ((0, "ub", 2 * B_WIDTH), (BCOL_Z, "zb", B_WIDTH))
_C_LAYOUT = ((0, "qkv_c", 3 * C_WIDTH), (CCOL_Z, "zc", C_WIDTH))


def _prep_kernel(w_ref, a_ref, b_ref, c_ref):
    for out_ref, layout in ((a_ref, _A_LAYOUT), (b_ref, _B_LAYOUT), (c_ref, _C_LAYOUT)):
        for dst, name, width in layout:
            src = _IN_START[name]
            if src % LANES == 0:
                out_ref[:, dst:dst + width] = w_ref[:, src:src + width].astype(BF16)
            else:
                base = src - src % LANES
                assert src + width == IN_COLS
                tail = w_ref[:, base:IN_COLS]
                out_ref[:, dst:dst + width] = tail[:, src - base:].astype(BF16)
    b0 = _IN_START["b_c"]
    assert b0 % LANES == 0 and b0 + LANES <= IN_COLS
    lane = lax.broadcasted_iota(jnp.int32, (w_ref.shape[0], LANES), 1)
    c_ref[:, CCOL_BA:CCOL_BA + LANES] = jnp.where(
        lane < 2 * C_HEADS, w_ref[:, b0:b0 + LANES], 0.0).astype(BF16)


def _prep_w_in(w_in, layer):
    _, d, cols = w_in.shape
    tr = 256
    widths = (A_COLS, B_COLS, C_COLS)
    return pl.pallas_call(
        _prep_kernel,
        grid=(d // tr,),
        in_specs=[pl.BlockSpec((None, tr, cols), lambda i: (layer, i, 0))],
        out_specs=[pl.BlockSpec((tr, width), lambda i: (i, 0)) for width in widths],
        out_shape=[jax.ShapeDtypeStruct((d, width), BF16) for width in widths],
        compiler_params=pltpu.CompilerParams(
            dimension_semantics=("parallel",), vmem_limit_bytes=VMEM_LIMIT),
        name="prep_w_in",
    )(w_in)


def _layer_parts(l, x2d, bsz, seq, prm, tabs=None):
    d = x2d.shape[1]
    if tabs is None:
        tabs = _rope_tables(prm["positions"])
    w_a, w_b, w_c = _prep_w_in(prm["w_in"], l)
    p2d, h2d = _in_proj(x2d, prm["norm_w"][l], w_a)
    p3 = p2d.reshape(bsz, seq, A_COLS)
    h3 = h2d.reshape(bsz, seq, d)
    ya = _attention(p3, tabs, prm["q_norm_w"][l], prm["k_norm_w"][l], prm["sinks"][l])
    yb = _conv_module(h3, w_b, prm["b_conv_w"][l], prm["b_conv_b"][l], prm["b_ln_w"][l],
                      prm["b_ln_b"][l], prm["b_pw_w"][l], prm["b_pw_b"][l])
    yc = _gated_deltanet(h3, w_c, prm["c_conv_w"][l], prm["c_a_log"][l],
                         prm["c_dt_bias"][l], prm["c_onorm_w"][l])
    xo = _out_proj(x2d, ya.reshape(bsz * seq, A_WIDTH), yb.reshape(bsz * seq, B_WIDTH),
                   yc.reshape(bsz * seq, C_WIDTH), prm["w_out"][l])
    return {"ya": ya, "yb": yb, "yc": yc, "xo": xo}


def kernel(x, positions, norm_w, w_in, q_norm_w, k_norm_w, sinks, b_conv_w, b_conv_b,
           b_ln_w, b_ln_b, b_pw_w, b_pw_b, c_conv_w, c_a_log, c_dt_bias, c_onorm_w, w_out):
    bsz, seq, d = x.shape
    prm = dict(positions=positions, norm_w=norm_w, w_in=w_in, q_norm_w=q_norm_w,
               k_norm_w=k_norm_w, sinks=sinks, b_conv_w=b_conv_w, b_conv_b=b_conv_b,
               b_ln_w=b_ln_w, b_ln_b=b_ln_b, b_pw_w=b_pw_w, b_pw_b=b_pw_b, c_conv_w=c_conv_w,
               c_a_log=c_a_log, c_dt_bias=c_dt_bias, c_onorm_w=c_onorm_w, w_out=w_out)
    tabs = _rope_tables(positions)
    x2d = x.reshape(bsz * seq, d)
    for l in range(w_in.shape[0]):
        x2d = _layer_parts(l, x2d, bsz, seq, prm, tabs)["xo"]
    return x2d.reshape(bsz, seq, d)
```
